```python
import math
import jax, jax.numpy as jnp
from jax import lax
import numpy as np

D_MODEL = 2048
BATCH = 8
SEQ = 4096
DEPTH = 2

D_FF = 5632
EPS = 1e-6
ATTN_HEAD_DIM = 64
ATTN_HEADS = D_MODEL // (2 * ATTN_HEAD_DIM)
ATTN_KV_HEADS = ATTN_HEADS // 4
ATTN_GROUP = ATTN_HEADS // ATTN_KV_HEADS
WINDOW = 128
ATTN_BLOCK = 128
REL_BUCKETS = 32
REL_MAX_DIST = 128
CONV_CH = D_MODEL // 2
CONV_WIDTH = 31
GLA_HEADS = 4
GLA_DV_HEAD = D_MODEL // 2 // GLA_HEADS
GLA_DK_HEAD = GLA_DV_HEAD // 2
GLA_CHUNK = 64
GLA_GATE_RANK = 16
GLA_GATE_NORM = 16.0
N_BRANCH = 3
ATTN_Q = ATTN_HEADS * ATTN_HEAD_DIM
ATTN_KV = ATTN_KV_HEADS * ATTN_HEAD_DIM
GLA_K = GLA_HEADS * GLA_DK_HEAD
GLA_V = GLA_HEADS * GLA_DV_HEAD
IN_SIZES = (ATTN_Q, ATTN_KV, ATTN_KV, 2 * CONV_CH, GLA_K, GLA_K, GLA_V, GLA_V, GLA_GATE_RANK, N_BRANCH * D_MODEL)
D_IN = sum(IN_SIZES)
SPLIT_IDX = tuple(int(v) for v in np.cumsum(IN_SIZES)[:-1])

kernel_name = "hybrid_gated_swa_conv_gla_macaron"


def rms_norm(x, g):
    xf = x.astype(jnp.float32)
    y = xf * lax.rsqrt(jnp.mean(xf * xf, axis=-1, keepdims=True) + EPS)
    return (y * g.astype(jnp.float32)).astype(x.dtype)


def layer_norm(x, g, b):
    xf = x.astype(jnp.float32)
    mu = jnp.mean(xf, axis=-1, keepdims=True)
    xc = xf - mu
    y = xc * lax.rsqrt(jnp.mean(xc * xc, axis=-1, keepdims=True) + EPS)
    return (y * g.astype(jnp.float32) + b.astype(jnp.float32)).astype(x.dtype)


def swiglu_ffn(h, w_gate, w_up, w_down):
    a = jnp.einsum('bsd,df->bsf', h, w_gate)
    u = jnp.einsum('bsd,df->bsf', h, w_up)
    return jnp.einsum('bsf,fd->bsd', jax.nn.silu(a) * u, w_down)


def t5_band_bias(rel_bias):
    qi = jnp.arange(ATTN_BLOCK)[:, None]
    kj = jnp.arange(2 * ATTN_BLOCK)[None, :]
    n = jnp.maximum(ATTN_BLOCK + qi - kj, 0)
    max_exact = REL_BUCKETS // 2
    nf = jnp.maximum(n, 1).astype(jnp.float32)
    large = max_exact + (jnp.log(nf / max_exact) / math.log(REL_MAX_DIST / max_exact)
                         * (REL_BUCKETS - max_exact)).astype(jnp.int32)
    large = jnp.minimum(large, REL_BUCKETS - 1)
    bucket = jnp.where(n < max_exact, n, large)
    return jnp.transpose(rel_bias[bucket], (2, 0, 1))


def sliding_window_attention(q, k, v, sink, rel_bias):
    B, S = q.shape[0], q.shape[1]
    nb = S // ATTN_BLOCK
    qb = q.reshape(B, nb, ATTN_BLOCK, ATTN_KV_HEADS, ATTN_GROUP, ATTN_HEAD_DIM)

    def band(t):
        tb = t.reshape(B, nb, ATTN_BLOCK, ATTN_KV_HEADS, ATTN_HEAD_DIM)
        prev = jnp.concatenate([jnp.zeros_like(tb[:, :1]), tb[:, :-1]], axis=1)
        return jnp.concatenate([prev, tb], axis=2)

    kband, vband = band(k), band(v)
    logits = jnp.einsum('bnqhgd,bnkhd->bnhgqk', qb, kband).astype(jnp.float32) * (ATTN_HEAD_DIM ** -0.5)
    bias = t5_band_bias(rel_bias).astype(jnp.float32).reshape(ATTN_KV_HEADS, ATTN_GROUP, ATTN_BLOCK, 2 * ATTN_BLOCK)
    logits = logits + bias[None, None]
    qi = jnp.arange(ATTN_BLOCK)[:, None]
    kj = jnp.arange(2 * ATTN_BLOCK)[None, :]
    dist = ATTN_BLOCK + qi - kj
    in_window = (dist >= 0) & (dist < WINDOW)
    has_prev = (jnp.arange(nb) > 0)[:, None, None] | (kj >= ATTN_BLOCK)[None]
    valid = in_window[None] & has_prev
    logits = jnp.where(valid[None, :, None, None], logits, -1e30)
    s = sink.astype(jnp.float32).reshape(ATTN_KV_HEADS, ATTN_GROUP)[None, None, :, :, None, None]
    m = jnp.maximum(jnp.max(logits, axis=-1, keepdims=True), s)
    p = jnp.exp(logits - m)
    probs = p / (jnp.sum(p, axis=-1, keepdims=True) + jnp.exp(s - m))
    out = jnp.einsum('bnhgqk,bnkhd->bnqhgd', probs.astype(v.dtype), vband)
    return out.reshape(B, S, ATTN_Q)


def conformer_conv(u, conv_w, conv_b, ln_g, ln_b):
    a, gate = jnp.split(u, 2, axis=-1)
    y = a * jax.nn.sigmoid(gate)
    y = lax.conv_general_dilated(y, conv_w[:, None, :].astype(y.dtype), window_strides=(1,),
                                 padding=[(CONV_WIDTH - 1, 0)],
                                 dimension_numbers=('NWC', 'WIO', 'NWC'),
                                 feature_group_count=CONV_CH)
    y = y + conv_b
    return jax.nn.silu(layer_norm(y, ln_g, ln_b))


def gla_chunked(q, k, v, gk):
    B, S, H, DK = q.shape
    DV = v.shape[-1]
    nc = S // GLA_CHUNK

    def chunks(t):
        return t.astype(jnp.float32).reshape(B, nc, GLA_CHUNK, H, t.shape[-1]).transpose(0, 1, 3, 2, 4)

    qc, kc, vc, gc = chunks(q), chunks(k), chunks(v), chunks(gk)
    b = jnp.cumsum(gc, axis=3)
    b_last = b[:, :, :, -1:, :]
    q_e = qc * (DK ** -0.5) * jnp.exp(b)
    k_e = kc * jnp.exp(-b)
    k_tail = kc * jnp.exp(b_last - b)
    causal = jnp.tril(jnp.ones((GLA_CHUNK, GLA_CHUNK), dtype=bool))
    att = jnp.where(causal, jnp.einsum('bnhtd,bnhsd->bnhts', q_e, k_e), 0.0)
    o_intra = jnp.einsum('bnhts,bnhsv->bnhtv', att, vc)
    kv = jnp.einsum('bnhsd,bnhsv->bnhdv', k_tail, vc)
    decay = jnp.exp(b_last[:, :, :, 0, :])

    def step(state, inp):
        dec, kvc = inp
        return dec[..., None] * state + kvc, state

    _, starts = lax.scan(step, jnp.zeros((B, H, DK, DV), jnp.float32),
                         (jnp.moveaxis(decay, 1, 0), jnp.moveaxis(kv, 1, 0)))
    starts = jnp.moveaxis(starts, 0, 1)
    o = o_intra + jnp.einsum('bnhtd,bnhdv->bnhtv', q_e, starts)
    return o.transpose(0, 1, 3, 2, 4).reshape(B, S, H, DV).astype(v.dtype)


def hybrid_mixer(h, w_in, attn_sink, rel_bias, conv_w, conv_b, conv_ln_g, conv_ln_b,
                 gla_gate_w2, gla_gate_b, gla_norm_g, w_a_up, w_b_up, w_c_up, w_out):
    B, S, _ = h.shape
    proj = jnp.einsum('bsd,dn->bsn', h, w_in)
    qa, ka, va, conv_in, qc, kc, vc, gc, gate_lr, gate_logits = jnp.split(proj, SPLIT_IDX, axis=-1)
    ya = sliding_window_attention(qa.reshape(B, S, ATTN_KV_HEADS, ATTN_GROUP, ATTN_HEAD_DIM),
                                  ka.reshape(B, S, ATTN_KV_HEADS, ATTN_HEAD_DIM),
                                  va.reshape(B, S, ATTN_KV_HEADS, ATTN_HEAD_DIM), attn_sink, rel_bias)
    yb = conformer_conv(conv_in, conv_w, conv_b, conv_ln_g, conv_ln_b)
    gk = jax.nn.log_sigmoid((jnp.einsum('bsr,rk->bsk', gate_lr, gla_gate_w2) + gla_gate_b).astype(jnp.float32)) / GLA_GATE_NORM
    oc = gla_chunked(qc.reshape(B, S, GLA_HEADS, GLA_DK_HEAD), kc.reshape(B, S, GLA_HEADS, GLA_DK_HEAD),
                     vc.reshape(B, S, GLA_HEADS, GLA_DV_HEAD), gk.reshape(B, S, GLA_HEADS, GLA_DK_HEAD))
    oc = rms_norm(oc, gla_norm_g) * jax.nn.silu(gc.reshape(B, S, GLA_HEADS, GLA_DV_HEAD))
    yc = oc.reshape(B, S, GLA_V)
    ya_d = jnp.einsum('bsc,cd->bsd', ya, w_a_up)
    yb_d = jnp.einsum('bsc,cd->bsd', yb, w_b_up)
    yc_d = jnp.einsum('bsc,cd->bsd', yc, w_c_up)
    gates = jax.nn.sigmoid(gate_logits).reshape(B, S, N_BRANCH, D_MODEL)
    merged = gates[:, :, 0] * ya_d + gates[:, :, 1] * yb_d + gates[:, :, 2] * yc_d
    return jnp.einsum('bsd,de->bse', merged, w_out)


def setup_inputs(seed: int = 0) -> dict:
    key = jax.random.key(seed)
    ks = jax.random.split(key, 32)

    def nrm(k, shape, fan_in):
        return jax.random.normal(k, shape, jnp.float32) * (fan_in ** -0.5)

    def gain(k, shape):
        return 1.0 + 0.02 * jax.random.normal(k, shape, jnp.float32)

    def small(k, shape, s=0.02):
        return s * jax.random.normal(k, shape, jnp.float32)

    L, D, F = DEPTH, D_MODEL, D_FF
    return {
        'x': jax.random.normal(ks[0], (BATCH, SEQ, D), jnp.float32),
        'rel_bias': small(ks[1], (REL_BUCKETS, ATTN_HEADS), 0.5),
        'ffn1_pre_g': gain(ks[2], (L, D)),
        'ffn1_post_g': gain(ks[3], (L, D)),
        'ffn1_w_gate': nrm(ks[4], (L, D, F), D),
        'ffn1_w_up': nrm(ks[5], (L, D, F), D),
        'ffn1_w_down': nrm(ks[6], (L, F, D), F),
        'mix_pre_g': gain(ks[7], (L, D)),
        'mix_post_g': gain(ks[8], (L, D)),
        'w_in': nrm(ks[9], (L, D, D_IN), D),
        'attn_sink': jax.random.normal(ks[10], (L, ATTN_HEADS), jnp.float32),
        'conv_w': nrm(ks[11], (L, CONV_WIDTH, CONV_CH), CONV_WIDTH),
        'conv_b': small(ks[12], (L, CONV_CH)),
        'conv_ln_g': gain(ks[13], (L, CONV_CH)),
        'conv_ln_b': small(ks[14], (L, CONV_CH)),
        'gla_gate_w2': nrm(ks[15], (L, GLA_GATE_RANK, GLA_K), GLA_GATE_RANK),
        'gla_gate_b': small(ks[16], (L, GLA_K), 0.1),
        'gla_norm_g': gain(ks[17], (L, GLA_DV_HEAD)),
        'w_a_up': nrm(ks[18], (L, ATTN_Q, D), ATTN_Q),
        'w_b_up': nrm(ks[19], (L, CONV_CH, D), CONV_CH),
        'w_c_up': nrm(ks[20], (L, GLA_V, D), GLA_V),
        'w_out': nrm(ks[21], (L, D, D), D),
        'ffn2_pre_g': gain(ks[22], (L, D)),
        'ffn2_post_g': gain(ks[23], (L, D)),
        'ffn2_w_gate': nrm(ks[24], (L, D, F), D),
        'ffn2_w_up': nrm(ks[25], (L, D, F), D),
        'ffn2_w_down': nrm(ks[26], (L, F, D), F),
    }


def reference(x, rel_bias, ffn1_pre_g, ffn1_post_g, ffn1_w_gate, ffn1_w_up, ffn1_w_down,
              mix_pre_g, mix_post_g, w_in, attn_sink, conv_w, conv_b, conv_ln_g, conv_ln_b,
              gla_gate_w2, gla_gate_b, gla_norm_g, w_a_up, w_b_up, w_c_up, w_out,
              ffn2_pre_g, ffn2_post_g, ffn2_w_gate, ffn2_w_up, ffn2_w_down):
    for l in range(DEPTH):
        f1 = swiglu_ffn(rms_norm(x, ffn1_pre_g[l]), ffn1_w_gate[l], ffn1_w_up[l], ffn1_w_down[l])
        x = x + 0.5 * rms_norm(f1, ffn1_post_g[l])
        m = hybrid_mixer(rms_norm(x, mix_pre_g[l]), w_in[l], attn_sink[l], rel_bias,
                         conv_w[l], conv_b[l], conv_ln_g[l], conv_ln_b[l],
                         gla_gate_w2[l], gla_gate_b[l], gla_norm_g[l],
                         w_a_up[l], w_b_up[l], w_c_up[l], w_out[l])
        x = x + rms_norm(m, mix_post_g[l])
        f2 = swiglu_ffn(rms_norm(x, ffn2_pre_g[l]), ffn2_w_gate[l], ffn2_w_up[l], ffn2_w_down[l])
        x = x + 0.5 * rms_norm(f2, ffn2_post_g[l])
    return x
```

```python
import functools
import math

import jax
import jax.numpy as jnp
from jax import lax
from jax.experimental import pallas as pl
from jax.experimental.pallas import tpu as pltpu

EPS = 1e-6
D_MODEL = 2048
D_FF = 5632
HEAD_DIM = 64
N_HEADS = 16
N_KV_HEADS = 4
ATTN_BLOCK = 128
REL_BUCKETS = 32
REL_MAX_DIST = 128
CONV_CH = 1024
CONV_WIDTH = 31
GLA_HEADS = 4
GLA_DK = 128
GLA_DV = 256
GLA_CHUNK = 64
GLA_RANK = 16
GLA_GATE_NORM = 16.0
MASK_VALUE = -1e30

LANES = 128
SUBLANES = 8
VMEM_LIMIT_BYTES = 56 * 1024 * 1024

COL_GATES = 0
COL_Q = 6144
COL_CONV_A = 7168
COL_CONV_G = 8192
COL_GLA_V = 9216
COL_GLA_G = 10240
COL_GLA_Q = 11264
COL_GLA_K = 11776
COL_K = 12288
COL_V = 12544
COL_LR = 12800
PROJ_WIDTH = 13312
PROJ_BLOCK_N = 1024

FFN_BLOCK_M = 512
FFN_BLOCK_F = 512
PROJ_BLOCK_M = 1024
ATTN_TOKENS = 1024
CONV_TOKENS = 256
CONV_HALO = 32
CONV_ROWS = 64
GLA_TOKENS = 512
MERGE_BLOCK_M = 256


def _params(*semantics):
    return pltpu.CompilerParams(dimension_semantics=semantics, vmem_limit_bytes=VMEM_LIMIT_BYTES)


def _rms(x, g):
    return x * lax.rsqrt(jnp.mean(x * x, axis=-1, keepdims=True) + EPS) * g


def _dot(a, b):
    return jnp.dot(a, b, preferred_element_type=jnp.float32)


def _dot_nt(a, b):
    return lax.dot_general(a, b, (((1,), (1,)), ((), ())), preferred_element_type=jnp.float32)


def _dot_tn(a, b):
    return lax.dot_general(a, b, (((0,), (0,)), ((), ())), preferred_element_type=jnp.float32)


def _ffn_kernel(x_ref, gpre_ref, gpost_ref, wg_ref, wu_ref, wd_ref, o_ref, h_scr, acc_scr):
    f = pl.program_id(1)

    @pl.when(f == 0)
    def _():
        h_scr[...] = _rms(x_ref[...], gpre_ref[...]).astype(jnp.bfloat16)

    h = h_scr[...]
    a = _dot(h, wg_ref[...])
    u = _dot(h, wu_ref[...])
    act = (a * jax.nn.sigmoid(a) * u).astype(jnp.bfloat16)
    part = _dot(act, wd_ref[...])

    @pl.when(f == 0)
    def _():
        acc_scr[...] = part

    @pl.when(f > 0)
    def _():
        acc_scr[...] += part

    @pl.when(f == pl.num_programs(1) - 1)
    def _():
        o_ref[...] = x_ref[...] + 0.5 * _rms(acc_scr[...], gpost_ref[...])


def _ffn(x, gpre, gpost, wg, wu, wd):
    nt, d = x.shape
    ff = wg.shape[1]
    bm = min(FFN_BLOCK_M, nt)
    bf = FFN_BLOCK_F
    return pl.pallas_call(
        _ffn_kernel,
        grid=(nt // bm, ff // bf),
        in_specs=[
            pl.BlockSpec((bm, d), lambda i, f: (i, 0)),
            pl.BlockSpec((1, d), lambda i, f: (0, 0)),
            pl.BlockSpec((1, d), lambda i, f: (0, 0)),
            pl.BlockSpec((d, bf), lambda i, f: (0, f)),
            pl.BlockSpec((d, bf), lambda i, f: (0, f)),
            pl.BlockSpec((bf, d), lambda i, f: (f, 0)),
        ],
        out_specs=pl.BlockSpec((bm, d), lambda i, f: (i, 0)),
        out_shape=jax.ShapeDtypeStruct((nt, d), jnp.float32),
        scratch_shapes=[pltpu.VMEM((bm, d), jnp.bfloat16), pltpu.VMEM((bm, d), jnp.float32)],
        compiler_params=_params("parallel", "arbitrary"),
        name="ffn",
    )(x, gpre, gpost, wg, wu, wd)


def _proj_kernel(x_ref, g_ref, w_ref, o_ref, h_scr):
    @pl.when(pl.program_id(1) == 0)
    def _():
        h_scr[...] = _rms(x_ref[...], g_ref[...]).astype(jnp.bfloat16)

    o_ref[...] = _dot(h_scr[...], w_ref[...]).astype(o_ref.dtype)


def _proj(x, g, w):
    nt, d = x.shape
    n = w.shape[1]
    bm = min(PROJ_BLOCK_M, nt)
    bn = PROJ_BLOCK_N
    return pl.pallas_call(
        _proj_kernel,
        grid=(nt // bm, n // bn),
        in_specs=[
            pl.BlockSpec((bm, d), lambda i, j: (i, 0)),
            pl.BlockSpec((1, d), lambda i, j: (0, 0)),
            pl.BlockSpec((d, bn), lambda i, j: (0, j)),
        ],
        out_specs=pl.BlockSpec((bm, bn), lambda i, j: (i, j)),
        out_shape=jax.ShapeDtypeStruct((nt, n), jnp.bfloat16),
        scratch_shapes=[pltpu.VMEM((bm, d), jnp.bfloat16)],
        compiler_params=_params("parallel", "arbitrary"),
        name="proj",
    )(x, g, w)


def _rel_buckets():
    qi = jnp.arange(ATTN_BLOCK)[:, None]
    kj = jnp.arange(2 * ATTN_BLOCK)[None, :]
    n = jnp.maximum(ATTN_BLOCK + qi - kj, 0)
    max_exact = REL_BUCKETS // 2
    nf = jnp.maximum(n, 1).astype(jnp.float32)
    large = max_exact + (jnp.log(nf / max_exact) / math.log(REL_MAX_DIST / max_exact)
                         * (REL_BUCKETS - max_exact)).astype(jnp.int32)
    large = jnp.minimum(large, REL_BUCKETS - 1)
    return jnp.where(n < max_exact, n, large).astype(jnp.int32)


def _bias_kernel(rel_ref, bucket_ref, o_ref):
    h = pl.program_id(0)
    bucket = bucket_ref[...]
    qi = lax.broadcasted_iota(jnp.int32, bucket.shape, 0)
    kj = lax.broadcasted_iota(jnp.int32, bucket.shape, 1)
    dist = ATTN_BLOCK + qi - kj
    acc = jnp.zeros(bucket.shape, jnp.float32)
    for b in range(REL_BUCKETS):
        acc = jnp.where(bucket == b, rel_ref[b, h], acc)
    o_ref[0] = jnp.where((dist >= 0) & (dist < ATTN_BLOCK), acc, MASK_VALUE)


def _band_bias(rel_bias):
    shape = (ATTN_BLOCK, 2 * ATTN_BLOCK)
    return pl.pallas_call(
        _bias_kernel,
        grid=(N_HEADS,),
        in_specs=[
            pl.BlockSpec(memory_space=pltpu.SMEM),
            pl.BlockSpec(shape, lambda h: (0, 0)),
        ],
        out_specs=pl.BlockSpec((1,) + shape, lambda h: (h, 0, 0)),
        out_shape=jax.ShapeDtypeStruct((N_HEADS,) + shape, jnp.float32),
        compiler_params=_params("arbitrary"),
        name="band_bias",
    )(rel_bias, _rel_buckets())


def _lane_swap_halves(x):
    y = pltpu.roll(pltpu.bitcast(x, jnp.uint32), HEAD_DIM, axis=1)
    return pltpu.bitcast(y, jnp.bfloat16)


def _head_pair_tiles(t128, head_in_high_lanes):
    lane = lax.broadcasted_iota(jnp.int32, t128.shape, 1)
    zero = jnp.zeros_like(t128)
    if head_in_high_lanes:
        hi = jnp.where(lane >= HEAD_DIM, t128, zero)
        return _lane_swap_halves(hi), hi
    lo = jnp.where(lane < HEAD_DIM, t128, zero)
    return lo, _lane_swap_halves(lo)


def _attn_kernel(sink_ref, q_ref, k_ref, v_ref, kp_ref, vp_ref, bias_ref, o_ref, kfull, vfull, *, steps_per_seq):
    blk = ATTN_BLOCK
    nblk = q_ref.shape[0] // blk
    first_step = pl.program_id(0) % steps_per_seq == 0
    kfull[0:blk, :] = kp_ref[...]
    vfull[0:blk, :] = vp_ref[...]
    kfull[blk:, :] = k_ref[...]
    vfull[blk:, :] = v_ref[...]
    key_in_prev = lax.broadcasted_iota(jnp.int32, (blk, 2 * blk), 1) < blk

    def block(n, carry):
        r0 = pl.multiple_of(n * blk, blk)
        no_prev = first_step & (n == 0)
        kband = kfull[pl.ds(r0, 2 * blk), :]
        vband = vfull[pl.ds(r0, 2 * blk), :]
        for j in range(N_KV_HEADS):
            c = (j // 2) * LANES
            ka, kb = _head_pair_tiles(kband[:, c:c + LANES], j % 2 == 1)
            va, vb = _head_pair_tiles(vband[:, c:c + LANES], j % 2 == 1)
            q0 = q_ref[pl.ds(r0, blk), j * 256:j * 256 + LANES]
            q1 = q_ref[pl.ds(r0, blk), j * 256 + LANES:(j + 1) * 256]
            qs = jnp.concatenate([q0, q1], axis=0) * (HEAD_DIM ** -0.5)
            sa = _dot_nt(qs, ka)
            sb = _dot_nt(qs, kb)
            probs = []
            for g, s in ((0, sa[:blk]), (1, sb[:blk]), (2, sa[blk:]), (3, sb[blk:])):
                h = j * 4 + g
                logits = s + bias_ref[h]
                logits = jnp.where(no_prev & key_in_prev, MASK_VALUE, logits)
                sink = sink_ref[h]
                m = jnp.maximum(jnp.max(logits, axis=-1, keepdims=True), sink)
                p = jnp.exp(logits - m)
                denom = jnp.sum(p, axis=-1, keepdims=True) + jnp.exp(sink - m)
                probs.append((p * (1.0 / denom)).astype(jnp.bfloat16))
            for pair in range(2):
                out = _dot(probs[2 * pair], va) + _dot(probs[2 * pair + 1], vb)
                col = j * 256 + pair * LANES
                o_ref[pl.ds(r0, blk), col:col + LANES] = out.astype(o_ref.dtype)
        return carry

    lax.fori_loop(0, nblk, block, 0)


def _attention(proj, sink, bias, seq):
    nt = proj.shape[0]
    tq = min(ATTN_TOKENS, seq)
    per = tq // ATTN_BLOCK
    steps_per_seq = seq // tq
    cq, ck, cv = COL_Q // 1024, COL_K // 256, COL_V // 256

    def prev(i):
        return jnp.maximum(i * per - 1, 0)

    return pl.pallas_call(
        functools.partial(_attn_kernel, steps_per_seq=steps_per_seq),
        grid=(nt // tq,),
        in_specs=[
            pl.BlockSpec(memory_space=pltpu.SMEM),
            pl.BlockSpec((tq, 1024), lambda i: (i, cq)),
            pl.BlockSpec((tq, 256), lambda i: (i, ck)),
            pl.BlockSpec((tq, 256), lambda i: (i, cv)),
            pl.BlockSpec((ATTN_BLOCK, 256), lambda i: (prev(i), ck)),
            pl.BlockSpec((ATTN_BLOCK, 256), lambda i: (prev(i), cv)),
            pl.BlockSpec((N_HEADS, ATTN_BLOCK, 2 * ATTN_BLOCK), lambda i: (0, 0, 0)),
        ],
        out_specs=pl.BlockSpec((tq, 1024), lambda i: (i, 0)),
        out_shape=jax.ShapeDtypeStruct((nt, 1024), jnp.bfloat16),
        scratch_shapes=[pltpu.VMEM((tq + ATTN_BLOCK, 256), jnp.bfloat16),
                        pltpu.VMEM((tq + ATTN_BLOCK, 256), jnp.bfloat16)],
        compiler_params=_params("arbitrary"),
        name="attention",
    )(sink, proj, proj, proj, proj, proj, bias)


def _conv_kernel(a_ref, g_ref, ap_ref, gp_ref, w_ref, b_ref, lng_ref, lnb_ref, o_ref, y_scr, c_scr, *, steps_per_seq):
    tm = a_ref.shape[0]
    halo = CONV_HALO
    first_step = pl.program_id(0) % steps_per_seq == 0
    yp = ap_ref[...].astype(jnp.float32) * jax.nn.sigmoid(gp_ref[...].astype(jnp.float32))
    y_scr[0, 0:halo, :] = jnp.where(first_step, 0.0, yp)
    y_scr[0, halo:, :] = a_ref[...].astype(jnp.float32) * jax.nn.sigmoid(g_ref[...].astype(jnp.float32))
    shifted_rows = tm + halo - SUBLANES
    for r in range(1, SUBLANES):
        y_scr[r, 0:shifted_rows, :] = y_scr[0, r:r + shifted_rows, :]
    base = halo - (CONV_WIDTH - 1)

    def rows(r, carry):
        r0 = pl.multiple_of(r * CONV_ROWS, CONV_ROWS)
        for c in range(CONV_CH // LANES):
            cs = slice(c * LANES, (c + 1) * LANES)
            acc = jnp.zeros((CONV_ROWS, LANES), jnp.float32)
            for j in range(CONV_WIDTH):
                phase, aligned = (base + j) % SUBLANES, (base + j) // SUBLANES * SUBLANES
                acc = acc + y_scr[phase, pl.ds(r0 + aligned, CONV_ROWS), cs] * w_ref[j:j + 1, cs]
            c_scr[pl.ds(r0, CONV_ROWS), cs] = acc + b_ref[:, cs]
        return carry

    lax.fori_loop(0, tm // CONV_ROWS, rows, 0)
    y = c_scr[...]
    mu = jnp.mean(y, axis=-1, keepdims=True)
    yc = y - mu
    z = yc * lax.rsqrt(jnp.mean(yc * yc, axis=-1, keepdims=True) + EPS) * lng_ref[...] + lnb_ref[...]
    o_ref[...] = (z * jax.nn.sigmoid(z)).astype(o_ref.dtype)


def _conv(proj, w, b, ln_g, ln_b, seq):
    nt = proj.shape[0]
    tm = min(CONV_TOKENS, seq)
    steps_per_seq = seq // tm
    per = tm // CONV_HALO
    ca, cg = COL_CONV_A // CONV_CH, COL_CONV_G // CONV_CH

    def prev(i):
        return jnp.maximum(i * per - 1, 0)

    vec = pl.BlockSpec((1, CONV_CH), lambda i: (0, 0))
    return pl.pallas_call(
        functools.partial(_conv_kernel, steps_per_seq=steps_per_seq),
        grid=(nt // tm,),
        in_specs=[
            pl.BlockSpec((tm, CONV_CH), lambda i: (i, ca)),
            pl.BlockSpec((tm, CONV_CH), lambda i: (i, cg)),
            pl.BlockSpec((CONV_HALO, CONV_CH), lambda i: (prev(i), ca)),
            pl.BlockSpec((CONV_HALO, CONV_CH), lambda i: (prev(i), cg)),
            pl.BlockSpec((CONV_WIDTH, CONV_CH), lambda i: (0, 0)),
            vec, vec, vec,
        ],
        out_specs=pl.BlockSpec((tm, CONV_CH), lambda i: (i, 0)),
        out_shape=jax.ShapeDtypeStruct((nt, CONV_CH), jnp.bfloat16),
        scratch_shapes=[pltpu.VMEM((SUBLANES, tm + CONV_HALO, CONV_CH), jnp.float32),
                        pltpu.VMEM((tm, CONV_CH), jnp.float32)],
        compiler_params=_params("arbitrary"),
        name="conformer_conv",
    )(proj, proj, proj, proj, w, b, ln_g, ln_b)


def _split_bf16(x):
    hi = x.astype(jnp.bfloat16)
    lo = (x - hi.astype(jnp.float32)).astype(jnp.bfloat16)
    return hi, lo


def _gla_kernel(q_ref, k_ref, v_ref, g_ref, lr_ref, w2_ref, gb_ref, ng_ref, o_ref, state, *, steps_per_seq):
    L = GLA_CHUNK
    nchunk = q_ref.shape[0] // L

    @pl.when(pl.program_id(0) % steps_per_seq == 0)
    def _():
        state[...] = jnp.zeros_like(state)

    row = lax.broadcasted_iota(jnp.int32, (L, L), 0)
    col = lax.broadcasted_iota(jnp.int32, (L, L), 1)
    causal = row >= col
    tri = causal.astype(jnp.bfloat16)
    ones = jnp.ones((L, GLA_DV), jnp.bfloat16)

    def chunk(c, carry):
        r0 = pl.multiple_of(c * L, L)
        rs = pl.ds(r0, L)
        z = _dot(lr_ref[rs, :], w2_ref[...]) + gb_ref[...]
        gk = (jnp.minimum(z, 0.0) - jnp.log1p(jnp.exp(-jnp.abs(z)))) / GLA_GATE_NORM
        gk_hi, gk_lo = _split_bf16(gk)
        b_all = _dot(tri, gk_hi) + _dot(tri, gk_lo)
        for h in range(GLA_HEADS):
            ks = slice(h * GLA_DK, (h + 1) * GLA_DK)
            vs = slice(h * GLA_DV, (h + 1) * GLA_DV)
            b = b_all[:, ks]
            b_last = b[L - 1:L, :]
            q = q_ref[rs, ks].astype(jnp.float32)
            k = k_ref[rs, ks].astype(jnp.float32)
            v = v_ref[rs, vs]
            q_e = (q * (GLA_DK ** -0.5) * jnp.exp(b)).astype(jnp.bfloat16)
            k_e = (k * jnp.exp(-b)).astype(jnp.bfloat16)
            k_tail = (k * jnp.exp(b_last - b)).astype(jnp.bfloat16)
            att = jnp.where(causal, _dot_nt(q_e, k_e), 0.0).astype(jnp.bfloat16)
            s_old = state[h]
            o = _dot(att, v) + _dot(q_e, s_old.astype(jnp.bfloat16))
            kv = _dot_tn(k_tail, v)
            b_last_rows = _dot_tn(gk_hi[:, ks], ones) + _dot_tn(gk_lo[:, ks], ones)
            state[h] = jnp.exp(b_last_rows) * s_old + kv
            o = _rms(o, ng_ref[...])
            gate = g_ref[rs, vs].astype(jnp.float32)
            o_ref[rs, vs] = (o * (gate * jax.nn.sigmoid(gate))).astype(o_ref.dtype)
        return carry

    lax.fori_loop(0, nchunk, chunk, 0)


def _gla(proj, w2, gate_b, norm_g, seq):
    nt = proj.shape[0]
    tm = min(GLA_TOKENS, seq)
    steps_per_seq = seq // tm
    return pl.pallas_call(
        functools.partial(_gla_kernel, steps_per_seq=steps_per_seq),
        grid=(nt // tm,),
        in_specs=[
            pl.BlockSpec((tm, 512), lambda i: (i, COL_GLA_Q // 512)),
            pl.BlockSpec((tm, 512), lambda i: (i, COL_GLA_K // 512)),
            pl.BlockSpec((tm, 1024), lambda i: (i, COL_GLA_V // 1024)),
            pl.BlockSpec((tm, 1024), lambda i: (i, COL_GLA_G // 1024)),
            pl.BlockSpec((tm, LANES), lambda i: (i, COL_LR // LANES)),
            pl.BlockSpec((LANES, 512), lambda i: (0, 0)),
            pl.BlockSpec((1, 512), lambda i: (0, 0)),
            pl.BlockSpec((1, GLA_DV), lambda i: (0, 0)),
        ],
        out_specs=pl.BlockSpec((tm, 1024), lambda i: (i, 0)),
        out_shape=jax.ShapeDtypeStruct((nt, 1024), jnp.bfloat16),
        scratch_shapes=[pltpu.VMEM((GLA_HEADS, GLA_DK, GLA_DV), jnp.float32)],
        compiler_params=_params("arbitrary"),
        name="gla",
    )(proj, proj, proj, proj, proj, w2, gate_b, norm_g)


def _merge_kernel(x_ref, ya_ref, yb_ref, yc_ref, ga_ref, gb_ref, gc_ref, wa_ref, wb_ref, wc_ref, wo_ref, g_ref, o_ref):
    def branch(gate_ref, y_ref, w_ref):
        return jax.nn.sigmoid(gate_ref[...].astype(jnp.float32)) * _dot(y_ref[...], w_ref[...])

    merged = branch(ga_ref, ya_ref, wa_ref) + branch(gb_ref, yb_ref, wb_ref) + branch(gc_ref, yc_ref, wc_ref)
    m = _dot(merged.astype(jnp.bfloat16), wo_ref[...])
    o_ref[...] = x_ref[...] + _rms(m, g_ref[...])


def _merge(x, ya, yb, yc, proj, wa, wb, wc, wo, g):
    nt, d = x.shape
    bm = min(MERGE_BLOCK_M, nt)
    c0 = COL_GATES // d

    def resident(shape):
        return pl.BlockSpec(shape, lambda i: (0, 0), pipeline_mode=pl.Buffered(1))

    def branch_in():
        return pl.BlockSpec((bm, 1024), lambda i: (i, 0))

    return pl.pallas_call(
        _merge_kernel,
        grid=(nt // bm,),
        in_specs=[
            pl.BlockSpec((bm, d), lambda i: (i, 0)),
            branch_in(), branch_in(), branch_in(),
            pl.BlockSpec((bm, d), lambda i: (i, c0)),
            pl.BlockSpec((bm, d), lambda i: (i, c0 + 1)),
            pl.BlockSpec((bm, d), lambda i: (i, c0 + 2)),
            resident((1024, d)), resident((1024, d)), resident((1024, d)), resident((d, d)),
            pl.BlockSpec((1, d), lambda i: (0, 0)),
        ],
        out_specs=pl.BlockSpec((bm, d), lambda i: (i, 0)),
        out_shape=jax.ShapeDtypeStruct((nt, d), jnp.float32),
        compiler_params=_params("parallel"),
        name="merge",
    )(x, ya, yb, yc, proj, proj, proj, wa, wb, wc, wo, g)


def _layout_w_in(w_in):
    d = w_in.shape[0]
    sizes = (1024, 256, 256, 2 * CONV_CH, 512, 512, 1024, 1024, GLA_RANK, 3 * D_MODEL)
    offs = [0]
    for s in sizes:
        offs.append(offs[-1] + s)
    qa, ka, va, conv, qc, kc, vc, gc, lr, gates = (w_in[:, offs[i]:offs[i + 1]] for i in range(len(sizes)))
    parts = [gates, qa, conv, vc, gc, qc, kc, ka, va, lr]
    used = sum(p.shape[1] for p in parts)
    parts.append(jnp.zeros((d, PROJ_WIDTH - used), w_in.dtype))
    return jnp.concatenate(parts, axis=1).astype(jnp.bfloat16)


def kernel(x, rel_bias, ffn1_pre_g, ffn1_post_g, ffn1_w_gate, ffn1_w_up, ffn1_w_down, mix_pre_g, mix_post_g, w_in, attn_sink, conv_w, conv_b, conv_ln_g, conv_ln_b, gla_gate_w2, gla_gate_b, gla_norm_g, w_a_up, w_b_up, w_c_up, w_out, ffn2_pre_g, ffn2_post_g, ffn2_w_gate, ffn2_w_up, ffn2_w_down):
    batch, seq, d = x.shape
    depth = w_in.shape[0]
    bf = lambda w: w.astype(jnp.bfloat16)
    row = lambda v: v.reshape(1, -1)
    bias = _band_bias(rel_bias)
    xt = x.reshape(batch * seq, d)
    for l in range(depth):
        xt = _ffn(xt, row(ffn1_pre_g[l]), row(ffn1_post_g[l]), bf(ffn1_w_gate[l]), bf(ffn1_w_up[l]), bf(ffn1_w_down[l]))
        proj = _proj(xt, row(mix_pre_g[l]), _layout_w_in(w_in[l]))
        ya = _attention(proj, attn_sink[l], bias, seq)
        yb = _conv(proj, conv_w[l], row(conv_b[l]), row(conv_ln_g[l]), row(conv_ln_b[l]), seq)
        w2 = jnp.zeros((LANES, GLA_HEADS * GLA_DK), jnp.bfloat16).at[:GLA_RANK].set(bf(gla_gate_w2[l]))
        yc = _gla(proj, w2, row(gla_gate_b[l]), row(gla_norm_g[l]), seq)
        xt = _merge(xt, ya, yb, yc, proj, bf(w_a_up[l]), bf(w_b_up[l]), bf(w_c_up[l]), bf(w_out[l]), row(mix_post_g[l]))
        xt = _ffn(xt, row(ffn2_pre_g[l]), row(ffn2_post_g[l]), bf(ffn2_w_gate[l]), bf(ffn2_w_up[l]), bf(ffn2_w_down[l]))
    return xt.reshape(batch, seq, d)
```

```python
import functools
import math

import jax
import jax.numpy as jnp
from jax import lax
from jax.experimental import pallas as pl
from jax.experimental.pallas import tpu as pltpu

EPS = 1e-6
D_MODEL = 2048
D_FF = 5632
HEAD_DIM = 64
N_HEADS = 16
N_KV_HEADS = 4
ATTN_BLOCK = 128
REL_BUCKETS = 32
REL_MAX_DIST = 128
CONV_CH = 1024
CONV_WIDTH = 31
GLA_HEADS = 4
GLA_DK = 128
GLA_DV = 256
GLA_CHUNK = 64
GLA_RANK = 16
GLA_GATE_NORM = 16.0
MASK_VALUE = -1e30

LANES = 128
SUBLANES = 8
VMEM_LIMIT_BYTES = 56 * 1024 * 1024

COL_GATES = 0
COL_Q = 6144
COL_CONV_A = 7168
COL_CONV_G = 8192
COL_GLA_V = 9216
COL_GLA_G = 10240
COL_GLA_Q = 11264
COL_GLA_K = 11776
COL_K = 12288
COL_V = 12544
COL_LR = 12800
PROJ_WIDTH = 13312
PROJ_BLOCK_N = 1024

FFN_BLOCK_M = 512
FFN_BLOCK_F = 512
FFN_SPLIT = 2
PROJ_BLOCK_M = 1024
ATTN_TOKENS = 1024
CONV_TOKENS = 256
CONV_HALO = 32
CONV_ROWS = 32
CONV_COLS = 512
GLA_TOKENS = 512
MERGE_BLOCK_M = 256


def _params(*semantics):
    return pltpu.CompilerParams(dimension_semantics=semantics, vmem_limit_bytes=VMEM_LIMIT_BYTES)


def _rms(x, g):
    return x * lax.rsqrt(jnp.mean(x * x, axis=-1, keepdims=True) + EPS) * g


def _dot(a, b):
    return jnp.dot(a, b, preferred_element_type=jnp.float32)


def _dot_nt(a, b):
    return lax.dot_general(a, b, (((1,), (1,)), ((), ())), preferred_element_type=jnp.float32)


def _dot_tn(a, b):
    return lax.dot_general(a, b, (((0,), (0,)), ((), ())), preferred_element_type=jnp.float32)


def _ffn_kernel(x_ref, gpre_ref, gpost_ref, wg_ref, wu_ref, wd_ref, o_ref, h_scr, acc_scr):
    f = pl.program_id(1)

    @pl.when(f == 0)
    def _():
        h_scr[...] = _rms(x_ref[...], gpre_ref[...]).astype(jnp.bfloat16)
        acc_scr[...] = jnp.zeros_like(acc_scr)

    h = h_scr[...]
    acts = []
    for s in range(FFN_SPLIT):
        cols = slice(s * wg_ref.shape[1] // FFN_SPLIT, (s + 1) * wg_ref.shape[1] // FFN_SPLIT)
        a = _dot(h, wg_ref[:, cols])
        u = _dot(h, wu_ref[:, cols])
        acts.append((a * jax.nn.sigmoid(a) * u).astype(jnp.bfloat16))
    acc_scr[...] += _dot(jnp.concatenate(acts, axis=1), wd_ref[...])

    @pl.when(f == pl.num_programs(1) - 1)
    def _():
        o_ref[...] = x_ref[...] + 0.5 * _rms(acc_scr[...], gpost_ref[...])


def _ffn(x, gpre, gpost, wg, wu, wd, l):
    nt, d = x.shape
    ff = wg.shape[2]
    bm = min(FFN_BLOCK_M, nt)
    bf = FFN_BLOCK_F
    return pl.pallas_call(
        _ffn_kernel,
        grid=(nt // bm, ff // bf),
        in_specs=[
            pl.BlockSpec((bm, d), lambda i, f: (i, 0)),
            pl.BlockSpec((1, d), lambda i, f: (0, 0)),
            pl.BlockSpec((1, d), lambda i, f: (0, 0)),
            pl.BlockSpec((None, d, bf), lambda i, f: (l, 0, f)),
            pl.BlockSpec((None, d, bf), lambda i, f: (l, 0, f)),
            pl.BlockSpec((None, bf, d), lambda i, f: (l, f, 0)),
        ],
        out_specs=pl.BlockSpec((bm, d), lambda i, f: (i, 0)),
        out_shape=jax.ShapeDtypeStruct((nt, d), jnp.float32),
        scratch_shapes=[pltpu.VMEM((bm, d), jnp.bfloat16), pltpu.VMEM((bm, d), jnp.float32)],
        compiler_params=_params("parallel", "arbitrary"),
        name="ffn",
    )(x, gpre, gpost, wg, wu, wd)


def _proj_kernel(x_ref, g_ref, w_ref, o_ref, h_scr):
    @pl.when(pl.program_id(1) == 0)
    def _():
        h_scr[...] = _rms(x_ref[...], g_ref[...]).astype(jnp.bfloat16)

    o_ref[...] = _dot(h_scr[...], w_ref[...]).astype(o_ref.dtype)


def _proj(x, g, w, l):
    nt, d = x.shape
    n = w.shape[2]
    bm = min(PROJ_BLOCK_M, nt)
    bn = PROJ_BLOCK_N
    return pl.pallas_call(
        _proj_kernel,
        grid=(nt // bm, n // bn),
        in_specs=[
            pl.BlockSpec((bm, d), lambda i, j: (i, 0)),
            pl.BlockSpec((1, d), lambda i, j: (0, 0)),
            pl.BlockSpec((None, d, bn), lambda i, j: (l, 0, j)),
        ],
        out_specs=pl.BlockSpec((bm, bn), lambda i, j: (i, j)),
        out_shape=jax.ShapeDtypeStruct((nt, n), jnp.bfloat16),
        scratch_shapes=[pltpu.VMEM((bm, d), jnp.bfloat16)],
        compiler_params=_params("parallel", "arbitrary"),
        name="proj",
    )(x, g, w)


def _rel_buckets():
    qi = jnp.arange(ATTN_BLOCK)[:, None]
    kj = jnp.arange(2 * ATTN_BLOCK)[None, :]
    n = jnp.maximum(ATTN_BLOCK + qi - kj, 0)
    max_exact = REL_BUCKETS // 2
    nf = jnp.maximum(n, 1).astype(jnp.float32)
    large = max_exact + (jnp.log(nf / max_exact) / math.log(REL_MAX_DIST / max_exact)
                         * (REL_BUCKETS - max_exact)).astype(jnp.int32)
    large = jnp.minimum(large, REL_BUCKETS - 1)
    return jnp.where(n < max_exact, n, large).astype(jnp.int32)


def _bias_kernel(rel_ref, bucket_ref, o_ref):
    h = pl.program_id(0)
    bucket = bucket_ref[...]
    qi = lax.broadcasted_iota(jnp.int32, bucket.shape, 0)
    kj = lax.broadcasted_iota(jnp.int32, bucket.shape, 1)
    dist = ATTN_BLOCK + qi - kj
    acc = jnp.zeros(bucket.shape, jnp.float32)
    for b in range(REL_BUCKETS):
        acc = jnp.where(bucket == b, rel_ref[b, h], acc)
    o_ref[0] = jnp.where((dist >= 0) & (dist < ATTN_BLOCK), acc, MASK_VALUE)


def _band_bias(rel_bias):
    shape = (ATTN_BLOCK, 2 * ATTN_BLOCK)
    return pl.pallas_call(
        _bias_kernel,
        grid=(N_HEADS,),
        in_specs=[
            pl.BlockSpec(memory_space=pltpu.SMEM),
            pl.BlockSpec(shape, lambda h: (0, 0)),
        ],
        out_specs=pl.BlockSpec((1,) + shape, lambda h: (h, 0, 0)),
        out_shape=jax.ShapeDtypeStruct((N_HEADS,) + shape, jnp.float32),
        compiler_params=_params("arbitrary"),
        name="band_bias",
    )(rel_bias, _rel_buckets())


def _lane_swap_halves(x):
    y = pltpu.roll(pltpu.bitcast(x, jnp.uint32), HEAD_DIM, axis=1)
    return pltpu.bitcast(y, jnp.bfloat16)


def _head_pair_tiles(t128, head_in_high_lanes):
    lane = lax.broadcasted_iota(jnp.int32, t128.shape, 1)
    zero = jnp.zeros_like(t128)
    if head_in_high_lanes:
        hi = jnp.where(lane >= HEAD_DIM, t128, zero)
        return _lane_swap_halves(hi), hi
    lo = jnp.where(lane < HEAD_DIM, t128, zero)
    return lo, _lane_swap_halves(lo)


def _attn_kernel(sink_ref, q_ref, k_ref, v_ref, kp_ref, vp_ref, bias_ref, o_ref, kfull, vfull, *, steps_per_seq):
    blk = ATTN_BLOCK
    nblk = q_ref.shape[0] // blk
    first_step = pl.program_id(0) % steps_per_seq == 0
    kfull[0:blk, :] = kp_ref[...]
    vfull[0:blk, :] = vp_ref[...]
    kfull[blk:, :] = k_ref[...]
    vfull[blk:, :] = v_ref[...]
    key_in_prev = lax.broadcasted_iota(jnp.int32, (blk, 2 * blk), 1) < blk

    def block(n, carry):
        r0 = pl.multiple_of(n * blk, blk)
        no_prev = first_step & (n == 0)
        kband = kfull[pl.ds(r0, 2 * blk), :]
        vband = vfull[pl.ds(r0, 2 * blk), :]
        for j in range(N_KV_HEADS):
            c = (j // 2) * LANES
            ka, kb = _head_pair_tiles(kband[:, c:c + LANES], j % 2 == 1)
            va, vb = _head_pair_tiles(vband[:, c:c + LANES], j % 2 == 1)
            q0 = q_ref[pl.ds(r0, blk), j * 256:j * 256 + LANES]
            q1 = q_ref[pl.ds(r0, blk), j * 256 + LANES:(j + 1) * 256]
            qs = jnp.concatenate([q0, q1], axis=0) * (HEAD_DIM ** -0.5)
            sa = _dot_nt(qs, ka)
            sb = _dot_nt(qs, kb)
            probs = []
            for g, s in ((0, sa[:blk]), (1, sb[:blk]), (2, sa[blk:]), (3, sb[blk:])):
                h = j * 4 + g
                logits = s + bias_ref[h]
                logits = jnp.where(no_prev & key_in_prev, MASK_VALUE, logits)
                sink = sink_ref[h]
                m = jnp.maximum(jnp.max(logits, axis=-1, keepdims=True), sink)
                p = jnp.exp(logits - m)
                denom = jnp.sum(p, axis=-1, keepdims=True) + jnp.exp(sink - m)
                probs.append((p * (1.0 / denom)).astype(jnp.bfloat16))
            for pair in range(2):
                out = _dot(probs[2 * pair], va) + _dot(probs[2 * pair + 1], vb)
                col = j * 256 + pair * LANES
                o_ref[pl.ds(r0, blk), col:col + LANES] = out.astype(o_ref.dtype)
        return carry

    lax.fori_loop(0, nblk, block, 0)


def _attention(proj, sink, bias, seq):
    nt = proj.shape[0]
    tq = min(ATTN_TOKENS, seq)
    per = tq // ATTN_BLOCK
    steps_per_seq = seq // tq
    cq, ck, cv = COL_Q // 1024, COL_K // 256, COL_V // 256

    def prev(i):
        return jnp.maximum(i * per - 1, 0)

    return pl.pallas_call(
        functools.partial(_attn_kernel, steps_per_seq=steps_per_seq),
        grid=(nt // tq,),
        in_specs=[
            pl.BlockSpec(memory_space=pltpu.SMEM),
            pl.BlockSpec((tq, 1024), lambda i: (i, cq)),
            pl.BlockSpec((tq, 256), lambda i: (i, ck)),
            pl.BlockSpec((tq, 256), lambda i: (i, cv)),
            pl.BlockSpec((ATTN_BLOCK, 256), lambda i: (prev(i), ck)),
            pl.BlockSpec((ATTN_BLOCK, 256), lambda i: (prev(i), cv)),
            pl.BlockSpec((N_HEADS, ATTN_BLOCK, 2 * ATTN_BLOCK), lambda i: (0, 0, 0)),
        ],
        out_specs=pl.BlockSpec((tq, 1024), lambda i: (i, 0)),
        out_shape=jax.ShapeDtypeStruct((nt, 1024), jnp.bfloat16),
        scratch_shapes=[pltpu.VMEM((tq + ATTN_BLOCK, 256), jnp.bfloat16),
                        pltpu.VMEM((tq + ATTN_BLOCK, 256), jnp.bfloat16)],
        compiler_params=_params("arbitrary"),
        name="attention",
    )(sink, proj, proj, proj, proj, proj, bias)


def _conv_kernel(a_ref, g_ref, ap_ref, gp_ref, w_ref, b_ref, lng_ref, lnb_ref, o_ref, y_scr, c_scr, wb_scr, *, steps_per_seq):
    tm = a_ref.shape[0]
    halo = CONV_HALO
    first_step = pl.program_id(0) % steps_per_seq == 0
    yp = ap_ref[...].astype(jnp.float32) * jax.nn.sigmoid(gp_ref[...].astype(jnp.float32))
    y_scr[0, 0:halo, :] = jnp.where(first_step, 0.0, yp)
    y_scr[0, halo:, :] = a_ref[...].astype(jnp.float32) * jax.nn.sigmoid(g_ref[...].astype(jnp.float32))
    shifted_rows = tm + halo - SUBLANES
    for r in range(1, SUBLANES):
        y_scr[r, 0:shifted_rows, :] = y_scr[0, r:r + shifted_rows, :]
    base = halo - (CONV_WIDTH - 1)

    @pl.when(pl.program_id(0) == 0)
    def _():
        for j in range(CONV_WIDTH):
            wb_scr[j] = jnp.broadcast_to(w_ref[j:j + 1, :], (SUBLANES, CONV_CH))
        wb_scr[CONV_WIDTH] = jnp.broadcast_to(b_ref[...], (SUBLANES, CONV_CH))

    lane_chunks = CONV_CH // CONV_COLS
    row_groups = CONV_ROWS // SUBLANES

    def tile(t, carry):
        r0 = pl.multiple_of((t // lane_chunks) * CONV_ROWS, CONV_ROWS)
        cs = pl.ds(pl.multiple_of((t % lane_chunks) * CONV_COLS, CONV_COLS), CONV_COLS)
        accs = [wb_scr[CONV_WIDTH, :, cs]] * row_groups
        for phase in range(SUBLANES):
            taps = [j for j in range(CONV_WIDTH) if (base + j) % SUBLANES == phase]
            first = [(base + j) // SUBLANES for j in taps]
            groups = {k: y_scr[phase, pl.ds(r0 + k * SUBLANES, SUBLANES), cs]
                      for k in range(min(first), max(first) + row_groups)}
            for j, k0 in zip(taps, first):
                wj = wb_scr[j, :, cs]
                for g in range(row_groups):
                    accs[g] = accs[g] + groups[k0 + g] * wj
        for g in range(row_groups):
            c_scr[pl.ds(r0 + g * SUBLANES, SUBLANES), cs] = accs[g]
        return carry

    lax.fori_loop(0, (tm // CONV_ROWS) * lane_chunks, tile, 0)
    y = c_scr[...]
    mu = jnp.mean(y, axis=-1, keepdims=True)
    yc = y - mu
    z = yc * lax.rsqrt(jnp.mean(yc * yc, axis=-1, keepdims=True) + EPS) * lng_ref[...] + lnb_ref[...]
    o_ref[...] = (z * jax.nn.sigmoid(z)).astype(o_ref.dtype)


def _conv(proj, w, b, ln_g, ln_b, seq):
    nt = proj.shape[0]
    tm = min(CONV_TOKENS, seq)
    steps_per_seq = seq // tm
    per = tm // CONV_HALO
    ca, cg = COL_CONV_A // CONV_CH, COL_CONV_G // CONV_CH

    def prev(i):
        return jnp.maximum(i * per - 1, 0)

    vec = pl.BlockSpec((1, CONV_CH), lambda i: (0, 0))
    return pl.pallas_call(
        functools.partial(_conv_kernel, steps_per_seq=steps_per_seq),
        grid=(nt // tm,),
        in_specs=[
            pl.BlockSpec((tm, CONV_CH), lambda i: (i, ca)),
            pl.BlockSpec((tm, CONV_CH), lambda i: (i, cg)),
            pl.BlockSpec((CONV_HALO, CONV_CH), lambda i: (prev(i), ca)),
            pl.BlockSpec((CONV_HALO, CONV_CH), lambda i: (prev(i), cg)),
            pl.BlockSpec((CONV_WIDTH, CONV_CH), lambda i: (0, 0)),
            vec, vec, vec,
        ],
        out_specs=pl.BlockSpec((tm, CONV_CH), lambda i: (i, 0)),
        out_shape=jax.ShapeDtypeStruct((nt, CONV_CH), jnp.bfloat16),
        scratch_shapes=[pltpu.VMEM((SUBLANES, tm + CONV_HALO, CONV_CH), jnp.float32),
                        pltpu.VMEM((tm, CONV_CH), jnp.float32),
                        pltpu.VMEM((CONV_WIDTH + 1, SUBLANES, CONV_CH), jnp.float32)],
        compiler_params=_params("arbitrary"),
        name="conformer_conv",
    )(proj, proj, proj, proj, w, b, ln_g, ln_b)


def _split_bf16(x):
    hi = x.astype(jnp.bfloat16)
    lo = (x - hi.astype(jnp.float32)).astype(jnp.bfloat16)
    return hi, lo


def _gla_kernel(q_ref, k_ref, v_ref, g_ref, lr_ref, w2_ref, gb_ref, ng_ref, o_ref, state, *, steps_per_seq):
    L = GLA_CHUNK
    nchunk = q_ref.shape[0] // L

    @pl.when(pl.program_id(0) % steps_per_seq == 0)
    def _():
        state[...] = jnp.zeros_like(state)

    row = lax.broadcasted_iota(jnp.int32, (L, L), 0)
    col = lax.broadcasted_iota(jnp.int32, (L, L), 1)
    causal = row >= col
    tri = causal.astype(jnp.bfloat16)
    ones = jnp.ones((L, GLA_DV), jnp.bfloat16)

    def chunk(c, carry):
        r0 = pl.multiple_of(c * L, L)
        rs = pl.ds(r0, L)
        z = _dot(lr_ref[rs, :], w2_ref[...]) + gb_ref[...]
        gk = (jnp.minimum(z, 0.0) - jnp.log1p(jnp.exp(-jnp.abs(z)))) / GLA_GATE_NORM
        gk_hi, gk_lo = _split_bf16(gk)
        b_all = _dot(tri, gk_hi) + _dot(tri, gk_lo)
        for h in range(GLA_HEADS):
            ks = slice(h * GLA_DK, (h + 1) * GLA_DK)
            vs = slice(h * GLA_DV, (h + 1) * GLA_DV)
            b = b_all[:, ks]
            b_last = b[L - 1:L, :]
            q = q_ref[rs, ks].astype(jnp.float32)
            k = k_ref[rs, ks].astype(jnp.float32)
            v = v_ref[rs, vs]
            q_e = (q * (GLA_DK ** -0.5) * jnp.exp(b)).astype(jnp.bfloat16)
            k_e = (k * jnp.exp(-b)).astype(jnp.bfloat16)
            k_tail = (k * jnp.exp(b_last - b)).astype(jnp.bfloat16)
            att = jnp.where(causal, _dot_nt(q_e, k_e), 0.0).astype(jnp.bfloat16)
            s_old = state[h]
            o = _dot(att, v) + _dot(q_e, s_old.astype(jnp.bfloat16))
            kv = _dot_tn(k_tail, v)
            b_last_rows = _dot_tn(gk_hi[:, ks], ones) + _dot_tn(gk_lo[:, ks], ones)
            state[h] = jnp.exp(b_last_rows) * s_old + kv
            o = _rms(o, ng_ref[...])
            gate = g_ref[rs, vs].astype(jnp.float32)
            o_ref[rs, vs] = (o * (gate * jax.nn.sigmoid(gate))).astype(o_ref.dtype)
        return carry

    lax.fori_loop(0, nchunk, chunk, 0)


def _gla(proj, w2, gate_b, norm_g, seq):
    nt = proj.shape[0]
    tm = min(GLA_TOKENS, seq)
    steps_per_seq = seq // tm
    return pl.pallas_call(
        functools.partial(_gla_kernel, steps_per_seq=steps_per_seq),
        grid=(nt // tm,),
        in_specs=[
            pl.BlockSpec((tm, 512), lambda i: (i, COL_GLA_Q // 512)),
            pl.BlockSpec((tm, 512), lambda i: (i, COL_GLA_K // 512)),
            pl.BlockSpec((tm, 1024), lambda i: (i, COL_GLA_V // 1024)),
            pl.BlockSpec((tm, 1024), lambda i: (i, COL_GLA_G // 1024)),
            pl.BlockSpec((tm, LANES), lambda i: (i, COL_LR // LANES)),
            pl.BlockSpec((LANES, 512), lambda i: (0, 0)),
            pl.BlockSpec((1, 512), lambda i: (0, 0)),
            pl.BlockSpec((1, GLA_DV), lambda i: (0, 0)),
        ],
        out_specs=pl.BlockSpec((tm, 1024), lambda i: (i, 0)),
        out_shape=jax.ShapeDtypeStruct((nt, 1024), jnp.bfloat16),
        scratch_shapes=[pltpu.VMEM((GLA_HEADS, GLA_DK, GLA_DV), jnp.float32)],
        compiler_params=_params("arbitrary"),
        name="gla",
    )(proj, proj, proj, proj, proj, w2, gate_b, norm_g)


def _merge_kernel(x_ref, ya_ref, yb_ref, yc_ref, ga_ref, gb_ref, gc_ref, wa_ref, wb_ref, wc_ref, wo_ref, g_ref, o_ref):
    def branch(gate_ref, y_ref, w_ref):
        return jax.nn.sigmoid(gate_ref[...].astype(jnp.float32)) * _dot(y_ref[...], w_ref[...])

    merged = branch(ga_ref, ya_ref, wa_ref) + branch(gb_ref, yb_ref, wb_ref) + branch(gc_ref, yc_ref, wc_ref)
    m = _dot(merged.astype(jnp.bfloat16), wo_ref[...])
    o_ref[...] = x_ref[...] + _rms(m, g_ref[...])


def _merge(x, ya, yb, yc, proj, wa, wb, wc, wo, g, l):
    nt, d = x.shape
    bm = min(MERGE_BLOCK_M, nt)
    c0 = COL_GATES // d

    def resident(shape):
        return pl.BlockSpec((None,) + shape, lambda i: (l, 0, 0), pipeline_mode=pl.Buffered(1))

    def branch_in():
        return pl.BlockSpec((bm, 1024), lambda i: (i, 0))

    return pl.pallas_call(
        _merge_kernel,
        grid=(nt // bm,),
        in_specs=[
            pl.BlockSpec((bm, d), lambda i: (i, 0)),
            branch_in(), branch_in(), branch_in(),
            pl.BlockSpec((bm, d), lambda i: (i, c0)),
            pl.BlockSpec((bm, d), lambda i: (i, c0 + 1)),
            pl.BlockSpec((bm, d), lambda i: (i, c0 + 2)),
            resident((1024, d)), resident((1024, d)), resident((1024, d)), resident((d, d)),
            pl.BlockSpec((1, d), lambda i: (0, 0)),
        ],
        out_specs=pl.BlockSpec((bm, d), lambda i: (i, 0)),
        out_shape=jax.ShapeDtypeStruct((nt, d), jnp.float32),
        compiler_params=_params("parallel"),
        name="merge",
    )(x, ya, yb, yc, proj, proj, proj, wa, wb, wc, wo, g)


def _layout_w_in(w_in):
    sizes = (1024, 256, 256, 2 * CONV_CH, 512, 512, 1024, 1024, GLA_RANK, 3 * D_MODEL)
    offs = [0]
    for s in sizes:
        offs.append(offs[-1] + s)
    qa, ka, va, conv, qc, kc, vc, gc, lr, gates = (w_in[..., offs[i]:offs[i + 1]] for i in range(len(sizes)))
    parts = [gates, qa, conv, vc, gc, qc, kc, ka, va, lr]
    used = sum(p.shape[-1] for p in parts)
    parts.append(jnp.zeros(w_in.shape[:-1] + (PROJ_WIDTH - used,), w_in.dtype))
    return jnp.concatenate(parts, axis=-1)


def kernel(x, rel_bias, ffn1_pre_g, ffn1_post_g, ffn1_w_gate, ffn1_w_up, ffn1_w_down, mix_pre_g, mix_post_g, w_in, attn_sink, conv_w, conv_b, conv_ln_g, conv_ln_b, gla_gate_w2, gla_gate_b, gla_norm_g, w_a_up, w_b_up, w_c_up, w_out, ffn2_pre_g, ffn2_post_g, ffn2_w_gate, ffn2_w_up, ffn2_w_down):
    batch, seq, d = x.shape
    depth = w_in.shape[0]
    bf = lambda w: w.astype(jnp.bfloat16)
    row = lambda v: v.reshape(1, -1)
    bias = _band_bias(rel_bias)
    f1g, f1u, f1d = bf(ffn1_w_gate), bf(ffn1_w_up), bf(ffn1_w_down)
    f2g, f2u, f2d = bf(ffn2_w_gate), bf(ffn2_w_up), bf(ffn2_w_down)
    win = _layout_w_in(bf(w_in))
    wa, wb, wc, wo = bf(w_a_up), bf(w_b_up), bf(w_c_up), bf(w_out)
    w2 = jnp.pad(bf(gla_gate_w2), ((0, 0), (0, LANES - GLA_RANK), (0, 0)))
    xt = x.reshape(batch * seq, d)
    for l in range(depth):
        xt = _ffn(xt, row(ffn1_pre_g[l]), row(ffn1_post_g[l]), f1g, f1u, f1d, l)
        proj = _proj(xt, row(mix_pre_g[l]), win, l)
        ya = _attention(proj, attn_sink[l], bias, seq)
        yb = _conv(proj, conv_w[l], row(conv_b[l]), row(conv_ln_g[l]), row(conv_ln_b[l]), seq)
        yc = _gla(proj, w2[l], row(gla_gate_b[l]), row(gla_norm_g[l]), seq)
        xt = _merge(xt, ya, yb, yc, proj, wa, wb, wc, wo, row(mix_post_g[l]), l)
        xt = _ffn(xt, row(ffn2_pre_g[l]), row(ffn2_post_g[l]), f2g, f2u, f2d, l)
    return xt.reshape(batch, seq, d)
```

```python
import functools
import math

import jax
import jax.numpy as jnp
from jax import lax
from jax.experimental import pallas as pl
from jax.experimental.pallas import tpu as pltpu

EPS = 1e-6
D_MODEL = 2048
D_FF = 5632
HEAD_DIM = 64
N_HEADS = 16
N_KV_HEADS = 4
ATTN_BLOCK = 128
REL_BUCKETS = 32
REL_MAX_DIST = 128
CONV_CH = 1024
CONV_WIDTH = 31
GLA_HEADS = 4
GLA_DK = 128
GLA_DV = 256
GLA_CHUNK = 64
GLA_RANK = 16
GLA_GATE_NORM = 16.0
MASK_VALUE = -1e30

LANES = 128
SUBLANES = 8
VMEM_LIMIT_BYTES = 56 * 1024 * 1024

COL_GATES = 0
COL_Q = 6144
COL_CONV_A = 7168
COL_CONV_G = 8192
COL_GLA_V = 9216
COL_GLA_G = 10240
COL_GLA_Q = 11264
COL_GLA_K = 11776
COL_K = 12288
COL_V = 12544
COL_LR = 12800
PROJ_WIDTH = 13312
PROJ_BLOCK_N = 1024

FFN_BLOCK_M = 512
FFN_BLOCK_F = 512
FFN_SPLIT = 2
PROJ_BLOCK_M = 1024
ATTN_TOKENS = 1024
CONV_TOKENS = 256
CONV_HALO = 32
CONV_ROWS = 32
CONV_COLS = 512
GLA_TOKENS = 512
GLA_UNROLL = 2
MERGE_BLOCK_M = 256


def _params(*semantics):
    return pltpu.CompilerParams(dimension_semantics=semantics, vmem_limit_bytes=VMEM_LIMIT_BYTES)


def _rms(x, g):
    return x * lax.rsqrt(jnp.mean(x * x, axis=-1, keepdims=True) + EPS) * g


def _dot(a, b):
    return jnp.dot(a, b, preferred_element_type=jnp.float32)


def _dot_nt(a, b):
    return lax.dot_general(a, b, (((1,), (1,)), ((), ())), preferred_element_type=jnp.float32)


def _dot_tn(a, b):
    return lax.dot_general(a, b, (((0,), (0,)), ((), ())), preferred_element_type=jnp.float32)


def _ffn_kernel(x_ref, gpre_ref, gpost_ref, wg_ref, wu_ref, wd_ref, o_ref, h_scr, acc_scr):
    f = pl.program_id(1)

    @pl.when(f == 0)
    def _():
        h_scr[...] = _rms(x_ref[...], gpre_ref[...]).astype(jnp.bfloat16)
        acc_scr[...] = jnp.zeros_like(acc_scr)

    h = h_scr[...]
    acts = []
    for s in range(FFN_SPLIT):
        cols = slice(s * wg_ref.shape[1] // FFN_SPLIT, (s + 1) * wg_ref.shape[1] // FFN_SPLIT)
        a = _dot(h, wg_ref[:, cols])
        u = _dot(h, wu_ref[:, cols])
        acts.append((a * jax.nn.sigmoid(a) * u).astype(jnp.bfloat16))
    acc_scr[...] += _dot(jnp.concatenate(acts, axis=1), wd_ref[...])

    @pl.when(f == pl.num_programs(1) - 1)
    def _():
        o_ref[...] = x_ref[...] + 0.5 * _rms(acc_scr[...], gpost_ref[...])


def _ffn(x, gpre, gpost, wg, wu, wd, l):
    nt, d = x.shape
    ff = wg.shape[2]
    bm = min(FFN_BLOCK_M, nt)
    bf = FFN_BLOCK_F
    return pl.pallas_call(
        _ffn_kernel,
        grid=(nt // bm, ff // bf),
        in_specs=[
            pl.BlockSpec((bm, d), lambda i, f: (i, 0)),
            pl.BlockSpec((1, d), lambda i, f: (0, 0)),
            pl.BlockSpec((1, d), lambda i, f: (0, 0)),
            pl.BlockSpec((None, d, bf), lambda i, f: (l, 0, f)),
            pl.BlockSpec((None, d, bf), lambda i, f: (l, 0, f)),
            pl.BlockSpec((None, bf, d), lambda i, f: (l, f, 0)),
        ],
        out_specs=pl.BlockSpec((bm, d), lambda i, f: (i, 0)),
        out_shape=jax.ShapeDtypeStruct((nt, d), jnp.float32),
        scratch_shapes=[pltpu.VMEM((bm, d), jnp.bfloat16), pltpu.VMEM((bm, d), jnp.float32)],
        compiler_params=_params("parallel", "arbitrary"),
        name="ffn",
    )(x, gpre, gpost, wg, wu, wd)


def _proj_kernel(x_ref, g_ref, w_ref, o_ref, h_scr):
    @pl.when(pl.program_id(1) == 0)
    def _():
        h_scr[...] = _rms(x_ref[...], g_ref[...]).astype(jnp.bfloat16)

    o_ref[...] = _dot(h_scr[...], w_ref[...]).astype(o_ref.dtype)


def _proj(x, g, w, l):
    nt, d = x.shape
    n = w.shape[2]
    bm = min(PROJ_BLOCK_M, nt)
    bn = PROJ_BLOCK_N
    return pl.pallas_call(
        _proj_kernel,
        grid=(nt // bm, n // bn),
        in_specs=[
            pl.BlockSpec((bm, d), lambda i, j: (i, 0)),
            pl.BlockSpec((1, d), lambda i, j: (0, 0)),
            pl.BlockSpec((None, d, bn), lambda i, j: (l, 0, j)),
        ],
        out_specs=pl.BlockSpec((bm, bn), lambda i, j: (i, j)),
        out_shape=jax.ShapeDtypeStruct((nt, n), jnp.bfloat16),
        scratch_shapes=[pltpu.VMEM((bm, d), jnp.bfloat16)],
        compiler_params=_params("parallel", "arbitrary"),
        name="proj",
    )(x, g, w)


def _rel_buckets():
    qi = jnp.arange(ATTN_BLOCK)[None, :]
    kj = jnp.arange(2 * ATTN_BLOCK)[:, None]
    n = jnp.maximum(ATTN_BLOCK + qi - kj, 0)
    max_exact = REL_BUCKETS // 2
    nf = jnp.maximum(n, 1).astype(jnp.float32)
    large = max_exact + (jnp.log(nf / max_exact) / math.log(REL_MAX_DIST / max_exact)
                         * (REL_BUCKETS - max_exact)).astype(jnp.int32)
    large = jnp.minimum(large, REL_BUCKETS - 1)
    return jnp.where(n < max_exact, n, large).astype(jnp.int32)


def _bias_kernel(rel_ref, bucket_ref, o_ref):
    h = pl.program_id(0)
    bucket = bucket_ref[...]
    kj = lax.broadcasted_iota(jnp.int32, bucket.shape, 0)
    qi = lax.broadcasted_iota(jnp.int32, bucket.shape, 1)
    dist = ATTN_BLOCK + qi - kj
    acc = jnp.zeros(bucket.shape, jnp.float32)
    for b in range(REL_BUCKETS):
        acc = jnp.where(bucket == b, rel_ref[b, h], acc)
    o_ref[0] = jnp.where((dist >= 0) & (dist < ATTN_BLOCK), acc, MASK_VALUE)


def _band_bias(rel_bias):
    shape = (2 * ATTN_BLOCK, ATTN_BLOCK)
    return pl.pallas_call(
        _bias_kernel,
        grid=(N_HEADS,),
        in_specs=[
            pl.BlockSpec(memory_space=pltpu.SMEM),
            pl.BlockSpec(shape, lambda h: (0, 0)),
        ],
        out_specs=pl.BlockSpec((1,) + shape, lambda h: (h, 0, 0)),
        out_shape=jax.ShapeDtypeStruct((N_HEADS,) + shape, jnp.float32),
        compiler_params=_params("arbitrary"),
        name="band_bias",
    )(rel_bias, _rel_buckets())


def _lane_swap_halves(x):
    y = pltpu.roll(pltpu.bitcast(x, jnp.uint32), HEAD_DIM, axis=1)
    return pltpu.bitcast(y, jnp.bfloat16)


def _head_pair_tiles(t128, head_in_high_lanes):
    lane = lax.broadcasted_iota(jnp.int32, t128.shape, 1)
    zero = jnp.zeros_like(t128)
    if head_in_high_lanes:
        hi = jnp.where(lane >= HEAD_DIM, t128, zero)
        return _lane_swap_halves(hi), hi
    lo = jnp.where(lane < HEAD_DIM, t128, zero)
    return lo, _lane_swap_halves(lo)


def _attn_kernel(sink_ref, q_ref, k_ref, v_ref, kp_ref, vp_ref, bias_ref, o_ref, kfull, vfull, s_even, s_odd, p_even, p_odd,
                 *, steps_per_seq):
    blk = ATTN_BLOCK
    nblk = q_ref.shape[0] // blk
    first_step = pl.program_id(0) % steps_per_seq == 0
    kfull[0:blk, :] = kp_ref[...]
    vfull[0:blk, :] = vp_ref[...]
    kfull[blk:, :] = k_ref[...]
    vfull[blk:, :] = v_ref[...]

    def first_row(n):
        return n * blk if isinstance(n, int) else pl.multiple_of(n * blk, blk)

    def scores(n, s_scr):
        r0 = first_row(n)
        kband = kfull[pl.ds(r0, 2 * blk), :]
        for j in range(N_KV_HEADS):
            c = (j // 2) * LANES
            ka, kb = _head_pair_tiles(kband[:, c:c + LANES], j % 2 == 1)
            q0 = q_ref[pl.ds(r0, blk), j * 256:j * 256 + LANES]
            q1 = q_ref[pl.ds(r0, blk), j * 256 + LANES:(j + 1) * 256]
            qs = jnp.concatenate([q0, q1], axis=0) * (HEAD_DIM ** -0.5)
            s_scr[j] = _dot_nt(jnp.concatenate([ka, kb], axis=0), qs)

    def softmax(s_scr, p_scr, maybe_no_prev=False):
        for h in range(N_HEADS):
            j, g = h // 4, h % 4
            logits = s_scr[j, (g % 2) * 2 * blk:(g % 2 + 1) * 2 * blk, (g // 2) * blk:(g // 2 + 1) * blk]
            logits = logits + bias_ref[h]
            if maybe_no_prev:
                prev_rows = jnp.where(first_step, MASK_VALUE, logits[:blk])
                logits = jnp.concatenate([prev_rows, logits[blk:]], axis=0)
            sink = sink_ref[h]
            m = jnp.maximum(jnp.max(logits, axis=0, keepdims=True), sink)
            p = jnp.exp(logits - m)
            key0 = lax.broadcasted_iota(jnp.int32, (SUBLANES, blk), 0) == 0
            top = jnp.where(key0, jnp.exp(sink - m), p[:SUBLANES])
            p_scr[h] = jnp.concatenate([top, p[SUBLANES:]], axis=0).astype(jnp.bfloat16)

    lane = lax.broadcasted_iota(jnp.int32, (2 * blk, LANES), 1)
    ones_even = (lane < HEAD_DIM).astype(jnp.bfloat16)
    ones_odd = (lane >= HEAD_DIM).astype(jnp.bfloat16)
    not_key0 = lax.broadcasted_iota(jnp.int32, (2 * blk, LANES), 0) > 0

    def values(n, p_scr):
        r0 = first_row(n)
        vband = vfull[pl.ds(r0, 2 * blk), :]
        for j in range(N_KV_HEADS):
            c = (j // 2) * LANES
            v_tile = jnp.where(not_key0, vband[:, c:c + LANES], jnp.zeros((2 * blk, LANES), jnp.bfloat16))
            va, vb = _head_pair_tiles(v_tile, j % 2 == 1)
            rhs_even = jnp.concatenate([va, ones_even], axis=1)
            rhs_odd = jnp.concatenate([vb, ones_odd], axis=1)
            for pair in range(2):
                h = j * 4 + 2 * pair
                out = _dot_tn(p_scr[h], rhs_even) + _dot_tn(p_scr[h + 1], rhs_odd)
                out = out[:, :LANES] * (1.0 / out[:, LANES:])
                col = j * 256 + pair * LANES
                o_ref[pl.ds(r0, blk), col:col + LANES] = out.astype(o_ref.dtype)

    def even_block(t, has_prev=True, maybe_no_prev=False):
        scores(t + 1, s_odd)
        softmax(s_even, p_even, maybe_no_prev)
        if has_prev:
            values(t - 1, p_odd)

    def odd_block(t, has_next=True):
        if has_next:
            scores(t + 1, s_even)
        softmax(s_odd, p_odd)
        values(t - 1, p_even)

    def block_pair(u, carry):
        t = 2 * u
        even_block(t)
        odd_block(t + 1)
        return carry

    scores(0, s_even)
    even_block(0, has_prev=False, maybe_no_prev=True)
    odd_block(1)
    lax.fori_loop(1, nblk // 2 - 1, block_pair, 0)
    even_block(nblk - 2)
    odd_block(nblk - 1, has_next=False)
    values(nblk - 1, p_odd)


def _attention(proj, sink, bias, seq):
    nt = proj.shape[0]
    tq = min(ATTN_TOKENS, seq)
    per = tq // ATTN_BLOCK
    assert per % 2 == 0 and per >= 4, "the block pipeline needs an even number (>= 4) of blocks per step"
    steps_per_seq = seq // tq
    cq, ck, cv = COL_Q // 1024, COL_K // 256, COL_V // 256

    def prev(i):
        return jnp.maximum(i * per - 1, 0)

    return pl.pallas_call(
        functools.partial(_attn_kernel, steps_per_seq=steps_per_seq),
        grid=(nt // tq,),
        in_specs=[
            pl.BlockSpec(memory_space=pltpu.SMEM),
            pl.BlockSpec((tq, 1024), lambda i: (i, cq)),
            pl.BlockSpec((tq, 256), lambda i: (i, ck)),
            pl.BlockSpec((tq, 256), lambda i: (i, cv)),
            pl.BlockSpec((ATTN_BLOCK, 256), lambda i: (prev(i), ck)),
            pl.BlockSpec((ATTN_BLOCK, 256), lambda i: (prev(i), cv)),
            pl.BlockSpec((N_HEADS, 2 * ATTN_BLOCK, ATTN_BLOCK), lambda i: (0, 0, 0)),
        ],
        out_specs=pl.BlockSpec((tq, 1024), lambda i: (i, 0)),
        out_shape=jax.ShapeDtypeStruct((nt, 1024), jnp.bfloat16),
        scratch_shapes=[pltpu.VMEM((tq + ATTN_BLOCK, 256), jnp.bfloat16),
                        pltpu.VMEM((tq + ATTN_BLOCK, 256), jnp.bfloat16),
                        pltpu.VMEM((N_KV_HEADS, 4 * ATTN_BLOCK, 2 * ATTN_BLOCK), jnp.float32),
                        pltpu.VMEM((N_KV_HEADS, 4 * ATTN_BLOCK, 2 * ATTN_BLOCK), jnp.float32),
                        pltpu.VMEM((N_HEADS, 2 * ATTN_BLOCK, ATTN_BLOCK), jnp.bfloat16),
                        pltpu.VMEM((N_HEADS, 2 * ATTN_BLOCK, ATTN_BLOCK), jnp.bfloat16)],
        compiler_params=_params("arbitrary"),
        name="attention",
    )(sink, proj, proj, proj, proj, proj, bias)


def _conv_kernel(a_ref, g_ref, ap_ref, gp_ref, w_ref, b_ref, lng_ref, lnb_ref, o_ref, y_scr, c_scr, wb_scr, *, steps_per_seq):
    tm = a_ref.shape[0]
    halo = CONV_HALO
    first_step = pl.program_id(0) % steps_per_seq == 0
    yp = ap_ref[...].astype(jnp.float32) * jax.nn.sigmoid(gp_ref[...].astype(jnp.float32))
    y_scr[0, 0:halo, :] = jnp.where(first_step, 0.0, yp)
    y_scr[0, halo:, :] = a_ref[...].astype(jnp.float32) * jax.nn.sigmoid(g_ref[...].astype(jnp.float32))
    shifted_rows = tm + halo - SUBLANES
    for r in range(1, SUBLANES):
        y_scr[r, 0:shifted_rows, :] = y_scr[0, r:r + shifted_rows, :]
    base = halo - (CONV_WIDTH - 1)

    @pl.when(pl.program_id(0) == 0)
    def _():
        for j in range(CONV_WIDTH):
            wb_scr[j] = jnp.broadcast_to(w_ref[j:j + 1, :], (SUBLANES, CONV_CH))
        wb_scr[CONV_WIDTH] = jnp.broadcast_to(b_ref[...], (SUBLANES, CONV_CH))

    lane_chunks = CONV_CH // CONV_COLS
    row_groups = CONV_ROWS // SUBLANES

    def tile(t, carry):
        r0 = pl.multiple_of((t // lane_chunks) * CONV_ROWS, CONV_ROWS)
        cs = pl.ds(pl.multiple_of((t % lane_chunks) * CONV_COLS, CONV_COLS), CONV_COLS)
        accs = [wb_scr[CONV_WIDTH, :, cs]] * row_groups
        for phase in range(SUBLANES):
            taps = [j for j in range(CONV_WIDTH) if (base + j) % SUBLANES == phase]
            first = [(base + j) // SUBLANES for j in taps]
            groups = {k: y_scr[phase, pl.ds(r0 + k * SUBLANES, SUBLANES), cs]
                      for k in range(min(first), max(first) + row_groups)}
            for j, k0 in zip(taps, first):
                wj = wb_scr[j, :, cs]
                for g in range(row_groups):
                    accs[g] = accs[g] + groups[k0 + g] * wj
        for g in range(row_groups):
            c_scr[pl.ds(r0 + g * SUBLANES, SUBLANES), cs] = accs[g]
        return carry

    lax.fori_loop(0, (tm // CONV_ROWS) * lane_chunks, tile, 0)
    y = c_scr[...]
    mu = jnp.mean(y, axis=-1, keepdims=True)
    yc = y - mu
    z = yc * lax.rsqrt(jnp.mean(yc * yc, axis=-1, keepdims=True) + EPS) * lng_ref[...] + lnb_ref[...]
    o_ref[...] = (z * jax.nn.sigmoid(z)).astype(o_ref.dtype)


def _conv(proj, w, b, ln_g, ln_b, seq):
    nt = proj.shape[0]
    tm = min(CONV_TOKENS, seq)
    steps_per_seq = seq // tm
    per = tm // CONV_HALO
    ca, cg = COL_CONV_A // CONV_CH, COL_CONV_G // CONV_CH

    def prev(i):
        return jnp.maximum(i * per - 1, 0)

    vec = pl.BlockSpec((1, CONV_CH), lambda i: (0, 0))
    return pl.pallas_call(
        functools.partial(_conv_kernel, steps_per_seq=steps_per_seq),
        grid=(nt // tm,),
        in_specs=[
            pl.BlockSpec((tm, CONV_CH), lambda i: (i, ca)),
            pl.BlockSpec((tm, CONV_CH), lambda i: (i, cg)),
            pl.BlockSpec((CONV_HALO, CONV_CH), lambda i: (prev(i), ca)),
            pl.BlockSpec((CONV_HALO, CONV_CH), lambda i: (prev(i), cg)),
            pl.BlockSpec((CONV_WIDTH, CONV_CH), lambda i: (0, 0)),
            vec, vec, vec,
        ],
        out_specs=pl.BlockSpec((tm, CONV_CH), lambda i: (i, 0)),
        out_shape=jax.ShapeDtypeStruct((nt, CONV_CH), jnp.bfloat16),
        scratch_shapes=[pltpu.VMEM((SUBLANES, tm + CONV_HALO, CONV_CH), jnp.float32),
                        pltpu.VMEM((tm, CONV_CH), jnp.float32),
                        pltpu.VMEM((CONV_WIDTH + 1, SUBLANES, CONV_CH), jnp.float32)],
        compiler_params=_params("arbitrary"),
        name="conformer_conv",
    )(proj, proj, proj, proj, w, b, ln_g, ln_b)


def _split_bf16(x):
    hi = x.astype(jnp.bfloat16)
    lo = (x - hi.astype(jnp.float32)).astype(jnp.bfloat16)
    return hi, lo


def _gla_kernel(q_ref, k_ref, v_ref, g_ref, lr_ref, w2_ref, gb_ref, ng_ref, o_ref,
                state_t, tri_scr, qe_scr, ke_scr, b_scr, oi_scr, kvt_scr, *, steps_per_seq):
    L = GLA_CHUNK
    tm = q_ref.shape[0]
    nchunk = tm // L

    @pl.when(pl.program_id(0) == 0)
    def _():
        row = lax.broadcasted_iota(jnp.int32, (tm, tm), 0)
        col = lax.broadcasted_iota(jnp.int32, (tm, tm), 1)
        tri_scr[...] = ((row // L == col // L) & (row >= col)).astype(jnp.bfloat16)

    @pl.when(pl.program_id(0) % steps_per_seq == 0)
    def _():
        state_t[...] = jnp.zeros_like(state_t)

    z = _dot(lr_ref[...], w2_ref[...]) + gb_ref[...]
    gk = (jnp.minimum(z, 0.0) - jnp.log(1.0 + jnp.exp(-jnp.abs(z)))) * (1.0 / GLA_GATE_NORM)
    gk_hi, gk_lo = _split_bf16(gk)
    b = _dot(tri_scr[...], gk_hi) + _dot(tri_scr[...], gk_lo)
    b_scr[...] = b
    qe_scr[...] = (q_ref[...].astype(jnp.float32) * (GLA_DK ** -0.5) * jnp.exp(b)).astype(jnp.bfloat16)
    ke_scr[...] = (k_ref[...].astype(jnp.float32) * jnp.exp(-b)).astype(jnp.bfloat16)

    causal = lax.broadcasted_iota(jnp.int32, (L, L), 0) >= lax.broadcasted_iota(jnp.int32, (L, L), 1)

    def local(c, carry):
        rs = pl.ds(pl.multiple_of(c * L, L), L)
        heads = [(slice(h * GLA_DK, (h + 1) * GLA_DK), slice(h * GLA_DV, (h + 1) * GLA_DV)) for h in range(GLA_HEADS)]
        scores = [_dot_nt(qe_scr[rs, ks], ke_scr[rs, ks]) for ks, _ in heads]
        for h, (ks, vs) in enumerate(heads):
            kvt_scr[c, h] = _dot_tn(v_ref[rs, vs], ke_scr[rs, ks])
        atts = [jnp.where(causal, s, 0.0).astype(jnp.bfloat16) for s in scores]
        for att, (_, vs) in zip(atts, heads):
            oi_scr[rs, vs] = _dot(att, v_ref[rs, vs])
        return carry

    lax.fori_loop(0, nchunk, local, 0, unroll=GLA_UNROLL)

    def recur(c, carry):
        r0 = pl.multiple_of(c * L, L)
        rs = pl.ds(r0, L)
        b_tail = b_scr[pl.ds(r0 + L - SUBLANES, SUBLANES), :]
        for h in range(GLA_HEADS):
            ks = slice(h * GLA_DK, (h + 1) * GLA_DK)
            vs = slice(h * GLA_DV, (h + 1) * GLA_DV)
            s_old = state_t[h]
            o = oi_scr[rs, vs] + _dot_nt(qe_scr[rs, ks], s_old.astype(jnp.bfloat16))
            decay = jnp.exp(b_tail[SUBLANES - 1:SUBLANES, ks])
            state_t[h] = decay * (s_old + kvt_scr[c, h])
            o = _rms(o, ng_ref[...])
            gate = g_ref[rs, vs].astype(jnp.float32)
            o_ref[rs, vs] = (o * (gate * jax.nn.sigmoid(gate))).astype(o_ref.dtype)
        return carry

    lax.fori_loop(0, nchunk, recur, 0, unroll=GLA_UNROLL)


def _gla(proj, w2, gate_b, norm_g, seq):
    nt = proj.shape[0]
    tm = min(GLA_TOKENS, seq)
    steps_per_seq = seq // tm
    return pl.pallas_call(
        functools.partial(_gla_kernel, steps_per_seq=steps_per_seq),
        grid=(nt // tm,),
        in_specs=[
            pl.BlockSpec((tm, 512), lambda i: (i, COL_GLA_Q // 512)),
            pl.BlockSpec((tm, 512), lambda i: (i, COL_GLA_K // 512)),
            pl.BlockSpec((tm, 1024), lambda i: (i, COL_GLA_V // 1024)),
            pl.BlockSpec((tm, 1024), lambda i: (i, COL_GLA_G // 1024)),
            pl.BlockSpec((tm, LANES), lambda i: (i, COL_LR // LANES)),
            pl.BlockSpec((LANES, 512), lambda i: (0, 0)),
            pl.BlockSpec((1, 512), lambda i: (0, 0)),
            pl.BlockSpec((1, GLA_DV), lambda i: (0, 0)),
        ],
        out_specs=pl.BlockSpec((tm, 1024), lambda i: (i, 0)),
        out_shape=jax.ShapeDtypeStruct((nt, 1024), jnp.bfloat16),
        scratch_shapes=[
            pltpu.VMEM((GLA_HEADS, GLA_DV, GLA_DK), jnp.float32),
            pltpu.VMEM((tm, tm), jnp.bfloat16),
            pltpu.VMEM((tm, GLA_HEADS * GLA_DK), jnp.bfloat16),
            pltpu.VMEM((tm, GLA_HEADS * GLA_DK), jnp.bfloat16),
            pltpu.VMEM((tm, GLA_HEADS * GLA_DK), jnp.float32),
            pltpu.VMEM((tm, GLA_HEADS * GLA_DV), jnp.float32),
            pltpu.VMEM((tm // GLA_CHUNK, GLA_HEADS, GLA_DV, GLA_DK), jnp.float32),
        ],
        compiler_params=_params("arbitrary"),
        name="gla",
    )(proj, proj, proj, proj, proj, w2, gate_b, norm_g)


def _merge_kernel(x_ref, ya_ref, yb_ref, yc_ref, ga_ref, gb_ref, gc_ref, wa_ref, wb_ref, wc_ref, wo_ref, g_ref, o_ref):
    def branch(gate_ref, y_ref, w_ref):
        return jax.nn.sigmoid(gate_ref[...].astype(jnp.float32)) * _dot(y_ref[...], w_ref[...])

    merged = branch(ga_ref, ya_ref, wa_ref) + branch(gb_ref, yb_ref, wb_ref) + branch(gc_ref, yc_ref, wc_ref)
    m = _dot(merged.astype(jnp.bfloat16), wo_ref[...])
    o_ref[...] = x_ref[...] + _rms(m, g_ref[...])


def _merge(x, ya, yb, yc, proj, wa, wb, wc, wo, g, l):
    nt, d = x.shape
    bm = min(MERGE_BLOCK_M, nt)
    c0 = COL_GATES // d

    def resident(shape):
        return pl.BlockSpec((None,) + shape, lambda i: (l, 0, 0), pipeline_mode=pl.Buffered(1))

    def branch_in():
        return pl.BlockSpec((bm, 1024), lambda i: (i, 0))

    return pl.pallas_call(
        _merge_kernel,
        grid=(nt // bm,),
        in_specs=[
            pl.BlockSpec((bm, d), lambda i: (i, 0)),
            branch_in(), branch_in(), branch_in(),
            pl.BlockSpec((bm, d), lambda i: (i, c0)),
            pl.BlockSpec((bm, d), lambda i: (i, c0 + 1)),
            pl.BlockSpec((bm, d), lambda i: (i, c0 + 2)),
            resident((1024, d)), resident((1024, d)), resident((1024, d)), resident((d, d)),
            pl.BlockSpec((1, d), lambda i: (0, 0)),
        ],
        out_specs=pl.BlockSpec((bm, d), lambda i: (i, 0)),
        out_shape=jax.ShapeDtypeStruct((nt, d), jnp.float32),
        compiler_params=_params("parallel"),
        name="merge",
    )(x, ya, yb, yc, proj, proj, proj, wa, wb, wc, wo, g)


def _layout_w_in(w_in):
    sizes = (1024, 256, 256, 2 * CONV_CH, 512, 512, 1024, 1024, GLA_RANK, 3 * D_MODEL)
    offs = [0]
    for s in sizes:
        offs.append(offs[-1] + s)
    qa, ka, va, conv, qc, kc, vc, gc, lr, gates = (w_in[..., offs[i]:offs[i + 1]] for i in range(len(sizes)))
    parts = [gates, qa, conv, vc, gc, qc, kc, ka, va, lr]
    used = sum(p.shape[-1] for p in parts)
    parts.append(jnp.zeros(w_in.shape[:-1] + (PROJ_WIDTH - used,), w_in.dtype))
    return jnp.concatenate(parts, axis=-1)


def kernel(x, rel_bias, ffn1_pre_g, ffn1_post_g, ffn1_w_gate, ffn1_w_up, ffn1_w_down, mix_pre_g, mix_post_g, w_in, attn_sink, conv_w, conv_b, conv_ln_g, conv_ln_b, gla_gate_w2, gla_gate_b, gla_norm_g, w_a_up, w_b_up, w_c_up, w_out, ffn2_pre_g, ffn2_post_g, ffn2_w_gate, ffn2_w_up, ffn2_w_down):
    batch, seq, d = x.shape
    depth = w_in.shape[0]
    bf = lambda w: w.astype(jnp.bfloat16)
    row = lambda v: v.reshape(1, -1)
    bias = _band_bias(rel_bias)
    f1g, f1u, f1d = bf(ffn1_w_gate), bf(ffn1_w_up), bf(ffn1_w_down)
    f2g, f2u, f2d = bf(ffn2_w_gate), bf(ffn2_w_up), bf(ffn2_w_down)
    win = bf(_layout_w_in(w_in))
    wa, wb, wc, wo = bf(w_a_up), bf(w_b_up), bf(w_c_up), bf(w_out)
    w2 = jnp.pad(bf(gla_gate_w2), ((0, 0), (0, LANES - GLA_RANK), (0, 0)))
    xt = x.reshape(batch * seq, d)
    for l in range(depth):
        xt = _ffn(xt, row(ffn1_pre_g[l]), row(ffn1_post_g[l]), f1g, f1u, f1d, l)
        proj = _proj(xt, row(mix_pre_g[l]), win, l)
        ya = _attention(proj, attn_sink[l], bias, seq)
        yb = _conv(proj, conv_w[l], row(conv_b[l]), row(conv_ln_g[l]), row(conv_ln_b[l]), seq)
        yc = _gla(proj, w2[l], row(gla_gate_b[l]), row(gla_norm_g[l]), seq)
        xt = _merge(xt, ya, yb, yc, proj, wa, wb, wc, wo, row(mix_post_g[l]), l)
        xt = _ffn(xt, row(ffn2_pre_g[l]), row(ffn2_post_g[l]), f2g, f2u, f2d, l)
    return xt.reshape(batch, seq, d)
```

```python
import functools
import math

import jax
import jax.numpy as jnp
from jax import lax
from jax.experimental import pallas as pl
from jax.experimental.pallas import tpu as pltpu

EPS = 1e-6
D_MODEL = 2048
D_FF = 5632
HEAD_DIM = 64
N_HEADS = 16
N_KV_HEADS = 4
ATTN_BLOCK = 128
REL_BUCKETS = 32
REL_MAX_DIST = 128
CONV_CH = 1024
CONV_WIDTH = 31
GLA_HEADS = 4
GLA_DK = 128
GLA_DV = 256
GLA_CHUNK = 64
GLA_RANK = 16
GLA_GATE_NORM = 16.0
MASK_VALUE = -1e30

LANES = 128
SUBLANES = 8
VMEM_LIMIT_BYTES = 56 * 1024 * 1024

COL_GATES = 0
COL_MAIN = 3 * D_MODEL
COL_Q = COL_MAIN
COL_K = COL_MAIN + 1024
COL_V = COL_MAIN + 1280
COL_CONV_A = COL_MAIN + 1536
COL_CONV_G = COL_MAIN + 2560
COL_GLA_Q = COL_MAIN + 3584
COL_GLA_K = COL_MAIN + 4096
COL_GLA_V = COL_MAIN + 4608
COL_GLA_G = COL_MAIN + 5632
COL_LR = COL_MAIN + 6656
MAIN_COLS = 6672
MAIN_WIDTH = 7168
HALF = 512
PROJ_BLOCK_N = 1024

FFN_BLOCK_M = 512
FFN_BLOCK_F = 512
FFN_SPLIT = 2
FFN_SLICE = 48
FFN_SLICE_ALIGN = 16
PROJ_BLOCK_M = 1024
ATTN_TOKENS = 1024
CONV_TOKENS = 512
CONV_HALO = 32
CONV_ROWS = 32
CONV_COLS = 512
GLA_TOKENS = 512
GLA_UNROLL = 2
MERGE_BLOCK_M = 256


def _params(*semantics):
    return pltpu.CompilerParams(dimension_semantics=semantics, vmem_limit_bytes=VMEM_LIMIT_BYTES)


def _rms(x, g):
    return x * lax.rsqrt(jnp.mean(x * x, axis=-1, keepdims=True) + EPS) * g


def _dot(a, b):
    return jnp.dot(a, b, preferred_element_type=jnp.float32)


def _dot_nt(a, b):
    return lax.dot_general(a, b, (((1,), (1,)), ((), ())), preferred_element_type=jnp.float32)


def _dot_tn(a, b):
    return lax.dot_general(a, b, (((0,), (0,)), ((), ())), preferred_element_type=jnp.float32)


def _ffn_kernel(xp_ref, xn_ref, gpre_ref, gpost_ref, wg_ref, wu_ref, wd_ref, o_ref, h_scr, acc_scr, *, nb):
    i, f = pl.program_id(0), pl.program_id(1)
    bm = xp_ref.shape[0]
    rows = pl.ds(pl.multiple_of(jnp.minimum(f * FFN_SLICE, bm - FFN_SLICE), FFN_SLICE_ALIGN), FFN_SLICE)

    @pl.when((i == 0) & (f == 0))
    def _():
        h_scr[0] = _rms(xp_ref[...], gpre_ref[...]).astype(jnp.bfloat16)
        acc_scr[1] = jnp.zeros(acc_scr.shape[1:], acc_scr.dtype)

    def finish(slot):
        o_ref[rows, :] = xp_ref[rows, :] + 0.5 * _rms(acc_scr[slot, rows, :], gpost_ref[...])

    def prepare(slot):
        h_scr[slot, rows, :] = _rms(xn_ref[rows, :], gpre_ref[...]).astype(jnp.bfloat16)

    def matmuls(slot):
        h = h_scr[slot]
        acts = []
        for s in range(FFN_SPLIT):
            cols = slice(s * wg_ref.shape[1] // FFN_SPLIT, (s + 1) * wg_ref.shape[1] // FFN_SPLIT)
            a = _dot(h, wg_ref[:, cols])
            u = _dot(h, wu_ref[:, cols])
            acts.append((a * jax.nn.sigmoid(a) * u).astype(jnp.bfloat16))
        carried = jnp.where(f > 0, acc_scr[slot], 0.0)
        acc_scr[slot] = carried + _dot(jnp.concatenate(acts, axis=1), wd_ref[...])

    for parity in (0, 1):
        @pl.when((i < nb) & (i % 2 == parity))
        def _():
            finish(1 - parity)
            matmuls(parity)
            prepare(1 - parity)

    @pl.when(i == nb)
    def _():
        finish((nb - 1) % 2)


def _ffn(x, gpre, gpost, wg, wu, wd, l):
    nt, d = x.shape
    ff = wg.shape[2]
    bm = min(FFN_BLOCK_M, nt)
    bf = FFN_BLOCK_F
    nb, nf = nt // bm, ff // bf
    assert FFN_SLICE * nf >= bm and bm % FFN_SLICE_ALIGN == 0 and FFN_SLICE % FFN_SLICE_ALIGN == 0

    def chunk(i, f):
        return jnp.where(i == nb, nf - 1, f)

    return pl.pallas_call(
        functools.partial(_ffn_kernel, nb=nb),
        grid=(nb + 1, nf),
        in_specs=[
            pl.BlockSpec((bm, d), lambda i, f: (jnp.maximum(i - 1, 0), 0)),
            pl.BlockSpec((bm, d), lambda i, f: (jnp.minimum(i + 1, nb - 1), 0)),
            pl.BlockSpec((1, d), lambda i, f: (0, 0)),
            pl.BlockSpec((1, d), lambda i, f: (0, 0)),
            pl.BlockSpec((None, d, bf), lambda i, f: (l, 0, chunk(i, f))),
            pl.BlockSpec((None, d, bf), lambda i, f: (l, 0, chunk(i, f))),
            pl.BlockSpec((None, bf, d), lambda i, f: (l, chunk(i, f), 0)),
        ],
        out_specs=pl.BlockSpec((bm, d), lambda i, f: (jnp.maximum(i - 1, 0), 0)),
        out_shape=jax.ShapeDtypeStruct((nt, d), jnp.float32),
        scratch_shapes=[pltpu.VMEM((2, bm, d), jnp.bfloat16), pltpu.VMEM((2, bm, d), jnp.float32)],
        compiler_params=_params("arbitrary", "arbitrary"),
        name="ffn",
    )(x, x, gpre, gpost, wg, wu, wd)


def _proj_kernel(x_ref, g_ref, wg_ref, wm_ref, o_ref, h_scr, *, gate_blocks):
    j = pl.program_id(1)

    @pl.when(j == 0)
    def _():
        h_scr[...] = _rms(x_ref[...], g_ref[...]).astype(jnp.bfloat16)

    @pl.when(j < gate_blocks)
    def _():
        o_ref[...] = _dot(h_scr[...], wg_ref[...]).astype(o_ref.dtype)

    @pl.when(j >= gate_blocks)
    def _():
        o_ref[...] = _dot(h_scr[...], wm_ref[...]).astype(o_ref.dtype)


def _proj(x, g, w_gates, w_main, l):
    nt, d = x.shape
    bm = min(PROJ_BLOCK_M, nt)
    bn = PROJ_BLOCK_N
    gate_blocks, main_blocks = w_gates.shape[2] // bn, w_main.shape[2] // bn
    return pl.pallas_call(
        functools.partial(_proj_kernel, gate_blocks=gate_blocks),
        grid=(nt // bm, gate_blocks + main_blocks),
        in_specs=[
            pl.BlockSpec((bm, d), lambda i, j: (i, 0)),
            pl.BlockSpec((1, d), lambda i, j: (0, 0)),
            pl.BlockSpec((None, d, bn), lambda i, j: (l, 0, jnp.minimum(j, gate_blocks - 1))),
            pl.BlockSpec((None, d, bn), lambda i, j: (l, 0, jnp.maximum(j - gate_blocks, 0))),
        ],
        out_specs=pl.BlockSpec((bm, bn), lambda i, j: (i, j)),
        out_shape=jax.ShapeDtypeStruct((nt, (gate_blocks + main_blocks) * bn), jnp.bfloat16),
        scratch_shapes=[pltpu.VMEM((bm, d), jnp.bfloat16)],
        compiler_params=_params("parallel", "arbitrary"),
        name="proj",
    )(x, g, w_gates, w_main)


def _rel_buckets():
    qi = jnp.arange(ATTN_BLOCK)[None, :]
    kj = jnp.arange(2 * ATTN_BLOCK)[:, None]
    n = jnp.maximum(ATTN_BLOCK + qi - kj, 0)
    max_exact = REL_BUCKETS // 2
    nf = jnp.maximum(n, 1).astype(jnp.float32)
    large = max_exact + (jnp.log(nf / max_exact) / math.log(REL_MAX_DIST / max_exact)
                         * (REL_BUCKETS - max_exact)).astype(jnp.int32)
    large = jnp.minimum(large, REL_BUCKETS - 1)
    return jnp.where(n < max_exact, n, large).astype(jnp.int32)


def _bias_kernel(rel_ref, bucket_ref, o_ref):
    h = pl.program_id(0)
    bucket = bucket_ref[...]
    kj = lax.broadcasted_iota(jnp.int32, bucket.shape, 0)
    qi = lax.broadcasted_iota(jnp.int32, bucket.shape, 1)
    dist = ATTN_BLOCK + qi - kj
    acc = jnp.zeros(bucket.shape, jnp.float32)
    for b in range(REL_BUCKETS):
        acc = jnp.where(bucket == b, rel_ref[b, h], acc)
    o_ref[0] = jnp.where((dist >= 0) & (dist < ATTN_BLOCK), acc, MASK_VALUE)


def _band_bias(rel_bias):
    shape = (2 * ATTN_BLOCK, ATTN_BLOCK)
    return pl.pallas_call(
        _bias_kernel,
        grid=(N_HEADS,),
        in_specs=[
            pl.BlockSpec(memory_space=pltpu.SMEM),
            pl.BlockSpec(shape, lambda h: (0, 0)),
        ],
        out_specs=pl.BlockSpec((1,) + shape, lambda h: (h, 0, 0)),
        out_shape=jax.ShapeDtypeStruct((N_HEADS,) + shape, jnp.float32),
        compiler_params=_params("arbitrary"),
        name="band_bias",
    )(rel_bias, _rel_buckets())


def _lane_swap_halves(x):
    y = pltpu.roll(pltpu.bitcast(x, jnp.uint32), HEAD_DIM, axis=1)
    return pltpu.bitcast(y, jnp.bfloat16)


def _head_pair_tiles(t128, head_in_high_lanes):
    lane = lax.broadcasted_iota(jnp.int32, t128.shape, 1)
    zero = jnp.zeros_like(t128)
    if head_in_high_lanes:
        hi = jnp.where(lane >= HEAD_DIM, t128, zero)
        return _lane_swap_halves(hi), hi
    lo = jnp.where(lane < HEAD_DIM, t128, zero)
    return lo, _lane_swap_halves(lo)


def _attn_kernel(sink_ref, q_ref, k_ref, v_ref, kp_ref, vp_ref, bias_ref, o_ref, kfull, vfull, s_even, s_odd, p_even, p_odd,
                 *, steps_per_seq):
    blk = ATTN_BLOCK
    nblk = q_ref.shape[0] // blk
    first_step = pl.program_id(0) % steps_per_seq == 0
    kfull[0:blk, :] = kp_ref[...]
    vfull[0:blk, :] = vp_ref[...]
    kfull[blk:, :] = k_ref[...]
    vfull[blk:, :] = v_ref[...]

    def first_row(n):
        return n * blk if isinstance(n, int) else pl.multiple_of(n * blk, blk)

    def scores(n, s_scr):
        r0 = first_row(n)
        kband = kfull[pl.ds(r0, 2 * blk), :]
        for j in range(N_KV_HEADS):
            c = (j // 2) * LANES
            ka, kb = _head_pair_tiles(kband[:, c:c + LANES], j % 2 == 1)
            q0 = q_ref[pl.ds(r0, blk), j * 256:j * 256 + LANES]
            q1 = q_ref[pl.ds(r0, blk), j * 256 + LANES:(j + 1) * 256]
            qs = jnp.concatenate([q0, q1], axis=0) * (HEAD_DIM ** -0.5)
            s_scr[j] = _dot_nt(jnp.concatenate([ka, kb], axis=0), qs)

    def softmax(s_scr, p_scr, maybe_no_prev=False):
        for h in range(N_HEADS):
            j, g = h // 4, h % 4
            logits = s_scr[j, (g % 2) * 2 * blk:(g % 2 + 1) * 2 * blk, (g // 2) * blk:(g // 2 + 1) * blk]
            logits = logits + bias_ref[h]
            if maybe_no_prev:
                prev_rows = jnp.where(first_step, MASK_VALUE, logits[:blk])
                logits = jnp.concatenate([prev_rows, logits[blk:]], axis=0)
            sink = sink_ref[h]
            m = jnp.maximum(jnp.max(logits, axis=0, keepdims=True), sink)
            p = jnp.exp(logits - m)
            key0 = lax.broadcasted_iota(jnp.int32, (SUBLANES, blk), 0) == 0
            top = jnp.where(key0, jnp.exp(sink - m), p[:SUBLANES])
            p_scr[h] = jnp.concatenate([top, p[SUBLANES:]], axis=0).astype(jnp.bfloat16)

    lane = lax.broadcasted_iota(jnp.int32, (2 * blk, LANES), 1)
    ones_even = (lane < HEAD_DIM).astype(jnp.bfloat16)
    ones_odd = (lane >= HEAD_DIM).astype(jnp.bfloat16)
    not_key0 = lax.broadcasted_iota(jnp.int32, (2 * blk, LANES), 0) > 0

    def values(n, p_scr):
        r0 = first_row(n)
        vband = vfull[pl.ds(r0, 2 * blk), :]
        for j in range(N_KV_HEADS):
            c = (j // 2) * LANES
            v_tile = jnp.where(not_key0, vband[:, c:c + LANES], jnp.zeros((2 * blk, LANES), jnp.bfloat16))
            va, vb = _head_pair_tiles(v_tile, j % 2 == 1)
            rhs_even = jnp.concatenate([va, ones_even], axis=1)
            rhs_odd = jnp.concatenate([vb, ones_odd], axis=1)
            for pair in range(2):
                h = j * 4 + 2 * pair
                out = _dot_tn(p_scr[h], rhs_even) + _dot_tn(p_scr[h + 1], rhs_odd)
                out = out[:, :LANES] * (1.0 / out[:, LANES:])
                col = j * 256 + pair * LANES
                o_ref[pl.ds(r0, blk), col:col + LANES] = out.astype(o_ref.dtype)

    def even_block(t, has_prev=True, maybe_no_prev=False):
        scores(t + 1, s_odd)
        softmax(s_even, p_even, maybe_no_prev)
        if has_prev:
            values(t - 1, p_odd)

    def odd_block(t, has_next=True):
        if has_next:
            scores(t + 1, s_even)
        softmax(s_odd, p_odd)
        values(t - 1, p_even)

    def block_pair(u, carry):
        t = 2 * u
        even_block(t)
        odd_block(t + 1)
        return carry

    scores(0, s_even)
    even_block(0, has_prev=False, maybe_no_prev=True)
    odd_block(1)
    lax.fori_loop(1, nblk // 2 - 1, block_pair, 0)
    even_block(nblk - 2)
    odd_block(nblk - 1, has_next=False)
    values(nblk - 1, p_odd)


def _attention(proj, sink, bias, seq):
    nt = proj.shape[0]
    tq = min(ATTN_TOKENS, seq)
    per = tq // ATTN_BLOCK
    assert per % 2 == 0 and per >= 4, "the block pipeline needs an even number (>= 4) of blocks per step"
    steps_per_seq = seq // tq
    cq, ck, cv = COL_Q // 1024, COL_K // 256, COL_V // 256

    def prev(i):
        return jnp.maximum(i * per - 1, 0)

    return pl.pallas_call(
        functools.partial(_attn_kernel, steps_per_seq=steps_per_seq),
        grid=(nt // tq,),
        in_specs=[
            pl.BlockSpec(memory_space=pltpu.SMEM),
            pl.BlockSpec((tq, 1024), lambda i: (i, cq)),
            pl.BlockSpec((tq, 256), lambda i: (i, ck)),
            pl.BlockSpec((tq, 256), lambda i: (i, cv)),
            pl.BlockSpec((ATTN_BLOCK, 256), lambda i: (prev(i), ck)),
            pl.BlockSpec((ATTN_BLOCK, 256), lambda i: (prev(i), cv)),
            pl.BlockSpec((N_HEADS, 2 * ATTN_BLOCK, ATTN_BLOCK), lambda i: (0, 0, 0)),
        ],
        out_specs=pl.BlockSpec((tq, 1024), lambda i: (i, 0)),
        out_shape=jax.ShapeDtypeStruct((nt, 1024), jnp.bfloat16),
        scratch_shapes=[pltpu.VMEM((tq + ATTN_BLOCK, 256), jnp.bfloat16),
                        pltpu.VMEM((tq + ATTN_BLOCK, 256), jnp.bfloat16),
                        pltpu.VMEM((N_KV_HEADS, 4 * ATTN_BLOCK, 2 * ATTN_BLOCK), jnp.float32),
                        pltpu.VMEM((N_KV_HEADS, 4 * ATTN_BLOCK, 2 * ATTN_BLOCK), jnp.float32),
                        pltpu.VMEM((N_HEADS, 2 * ATTN_BLOCK, ATTN_BLOCK), jnp.bfloat16),
                        pltpu.VMEM((N_HEADS, 2 * ATTN_BLOCK, ATTN_BLOCK), jnp.bfloat16)],
        compiler_params=_params("arbitrary"),
        name="attention",
    )(sink, proj, proj, proj, proj, proj, bias)


def _conv_kernel(a0_ref, a1_ref, g0_ref, g1_ref, ap0_ref, ap1_ref, gp0_ref, gp1_ref, w_ref, b_ref, lng_ref, lnb_ref,
                 o_ref, y_scr, c_scr, wb_scr, *, steps_per_seq):
    tm = a0_ref.shape[0]
    halo = CONV_HALO
    first_step = pl.program_id(0) % steps_per_seq == 0
    halves = ((a0_ref, g0_ref, ap0_ref, gp0_ref), (a1_ref, g1_ref, ap1_ref, gp1_ref))
    for half, (a_ref, g_ref, ap_ref, gp_ref) in enumerate(halves):
        hs = slice(half * HALF, (half + 1) * HALF)
        yp = ap_ref[...].astype(jnp.float32) * jax.nn.sigmoid(gp_ref[...].astype(jnp.float32))
        y_scr[0, 0:halo, hs] = jnp.where(first_step, 0.0, yp)
        y_scr[0, halo:, hs] = a_ref[...].astype(jnp.float32) * jax.nn.sigmoid(g_ref[...].astype(jnp.float32))
    shifted_rows = tm + halo - SUBLANES
    for r in range(1, SUBLANES):
        y_scr[r, 0:shifted_rows, :] = y_scr[0, r:r + shifted_rows, :]
    base = halo - (CONV_WIDTH - 1)

    @pl.when(pl.program_id(0) == 0)
    def _():
        for j in range(CONV_WIDTH):
            wb_scr[j] = jnp.broadcast_to(w_ref[j:j + 1, :], (SUBLANES, CONV_CH))
        wb_scr[CONV_WIDTH] = jnp.broadcast_to(b_ref[...], (SUBLANES, CONV_CH))

    lane_chunks = CONV_CH // CONV_COLS
    row_groups = CONV_ROWS // SUBLANES

    def tile(t, carry):
        r0 = pl.multiple_of((t // lane_chunks) * CONV_ROWS, CONV_ROWS)
        cs = pl.ds(pl.multiple_of((t % lane_chunks) * CONV_COLS, CONV_COLS), CONV_COLS)
        accs = [wb_scr[CONV_WIDTH, :, cs]] * row_groups
        for phase in range(SUBLANES):
            taps = [j for j in range(CONV_WIDTH) if (base + j) % SUBLANES == phase]
            first = [(base + j) // SUBLANES for j in taps]
            groups = {k: y_scr[phase, pl.ds(r0 + k * SUBLANES, SUBLANES), cs]
                      for k in range(min(first), max(first) + row_groups)}
            for j, k0 in zip(taps, first):
                wj = wb_scr[j, :, cs]
                for g in range(row_groups):
                    accs[g] = accs[g] + groups[k0 + g] * wj
        for g in range(row_groups):
            c_scr[pl.ds(r0 + g * SUBLANES, SUBLANES), cs] = accs[g]
        return carry

    lax.fori_loop(0, (tm // CONV_ROWS) * lane_chunks, tile, 0)
    y = c_scr[...]
    mu = jnp.mean(y, axis=-1, keepdims=True)
    yc = y - mu
    z = yc * lax.rsqrt(jnp.mean(yc * yc, axis=-1, keepdims=True) + EPS) * lng_ref[...] + lnb_ref[...]
    o_ref[...] = (z * jax.nn.sigmoid(z)).astype(o_ref.dtype)


def _conv(proj, w, b, ln_g, ln_b, seq):
    nt = proj.shape[0]
    tm = min(CONV_TOKENS, seq)
    steps_per_seq = seq // tm
    per = tm // CONV_HALO
    ca, cg = COL_CONV_A // HALF, COL_CONV_G // HALF

    def prev(i):
        return jnp.maximum(i * per - 1, 0)

    def cur(c):
        return pl.BlockSpec((tm, HALF), lambda i: (i, c))

    def before(c):
        return pl.BlockSpec((CONV_HALO, HALF), lambda i: (prev(i), c))

    vec = pl.BlockSpec((1, CONV_CH), lambda i: (0, 0))
    return pl.pallas_call(
        functools.partial(_conv_kernel, steps_per_seq=steps_per_seq),
        grid=(nt // tm,),
        in_specs=[
            cur(ca), cur(ca + 1), cur(cg), cur(cg + 1),
            before(ca), before(ca + 1), before(cg), before(cg + 1),
            pl.BlockSpec((CONV_WIDTH, CONV_CH), lambda i: (0, 0)),
            vec, vec, vec,
        ],
        out_specs=pl.BlockSpec((tm, CONV_CH), lambda i: (i, 0)),
        out_shape=jax.ShapeDtypeStruct((nt, CONV_CH), jnp.bfloat16),
        scratch_shapes=[pltpu.VMEM((SUBLANES, tm + CONV_HALO, CONV_CH), jnp.float32),
                        pltpu.VMEM((tm, CONV_CH), jnp.float32),
                        pltpu.VMEM((CONV_WIDTH + 1, SUBLANES, CONV_CH), jnp.float32)],
        compiler_params=_params("arbitrary"),
        name="conformer_conv",
    )(*([proj] * 8), w, b, ln_g, ln_b)


def _split_bf16(x):
    hi = x.astype(jnp.bfloat16)
    lo = (x - hi.astype(jnp.float32)).astype(jnp.bfloat16)
    return hi, lo


def _gla_kernel(q_ref, k_ref, v0_ref, v1_ref, g0_ref, g1_ref, lr_ref, w2_ref, gb_ref, ng_ref, o_ref,
                state_t, tri_scr, qe_scr, ke_scr, b_scr, oi_scr, kvt_scr, *, steps_per_seq):
    L = GLA_CHUNK
    tm = q_ref.shape[0]
    nchunk = tm // L

    @pl.when(pl.program_id(0) == 0)
    def _():
        row = lax.broadcasted_iota(jnp.int32, (tm, tm), 0)
        col = lax.broadcasted_iota(jnp.int32, (tm, tm), 1)
        tri_scr[...] = ((row // L == col // L) & (row >= col)).astype(jnp.bfloat16)

    @pl.when(pl.program_id(0) % steps_per_seq == 0)
    def _():
        state_t[...] = jnp.zeros_like(state_t)

    z = _dot(lr_ref[...], w2_ref[...]) + gb_ref[...]
    gk = (jnp.minimum(z, 0.0) - jnp.log(1.0 + jnp.exp(-jnp.abs(z)))) * (1.0 / GLA_GATE_NORM)
    gk_hi, gk_lo = _split_bf16(gk)
    b = _dot(tri_scr[...], gk_hi) + _dot(tri_scr[...], gk_lo)
    b_scr[...] = b
    qe_scr[...] = (q_ref[...].astype(jnp.float32) * (GLA_DK ** -0.5) * jnp.exp(b)).astype(jnp.bfloat16)
    ke_scr[...] = (k_ref[...].astype(jnp.float32) * jnp.exp(-b)).astype(jnp.bfloat16)

    causal = lax.broadcasted_iota(jnp.int32, (L, L), 0) >= lax.broadcasted_iota(jnp.int32, (L, L), 1)

    def head_cols(halves, rs, h):
        per_half = HALF // GLA_DV
        return halves[h // per_half][rs, (h % per_half) * GLA_DV:(h % per_half + 1) * GLA_DV]

    def local(c, carry):
        rs = pl.ds(pl.multiple_of(c * L, L), L)
        heads = [(slice(h * GLA_DK, (h + 1) * GLA_DK), slice(h * GLA_DV, (h + 1) * GLA_DV)) for h in range(GLA_HEADS)]
        scores = [_dot_nt(qe_scr[rs, ks], ke_scr[rs, ks]) for ks, _ in heads]
        for h, (ks, vs) in enumerate(heads):
            kvt_scr[c, h] = _dot_tn(head_cols((v0_ref, v1_ref), rs, h), ke_scr[rs, ks])
        atts = [jnp.where(causal, s, 0.0).astype(jnp.bfloat16) for s in scores]
        for h, (_, vs) in enumerate(heads):
            oi_scr[rs, vs] = _dot(atts[h], head_cols((v0_ref, v1_ref), rs, h))
        return carry

    lax.fori_loop(0, nchunk, local, 0, unroll=GLA_UNROLL)

    def recur(c, carry):
        r0 = pl.multiple_of(c * L, L)
        rs = pl.ds(r0, L)
        b_tail = b_scr[pl.ds(r0 + L - SUBLANES, SUBLANES), :]
        for h in range(GLA_HEADS):
            ks = slice(h * GLA_DK, (h + 1) * GLA_DK)
            vs = slice(h * GLA_DV, (h + 1) * GLA_DV)
            s_old = state_t[h]
            o = oi_scr[rs, vs] + _dot_nt(qe_scr[rs, ks], s_old.astype(jnp.bfloat16))
            decay = jnp.exp(b_tail[SUBLANES - 1:SUBLANES, ks])
            state_t[h] = decay * (s_old + kvt_scr[c, h])
            o = _rms(o, ng_ref[...])
            gate = head_cols((g0_ref, g1_ref), rs, h).astype(jnp.float32)
            o_ref[rs, vs] = (o * (gate * jax.nn.sigmoid(gate))).astype(o_ref.dtype)
        return carry

    lax.fori_loop(0, nchunk, recur, 0, unroll=GLA_UNROLL)


def _gla(proj, w2, gate_b, norm_g, seq):
    nt = proj.shape[0]
    tm = min(GLA_TOKENS, seq)
    steps_per_seq = seq // tm
    return pl.pallas_call(
        functools.partial(_gla_kernel, steps_per_seq=steps_per_seq),
        grid=(nt // tm,),
        in_specs=[
            pl.BlockSpec((tm, 512), lambda i: (i, COL_GLA_Q // 512)),
            pl.BlockSpec((tm, 512), lambda i: (i, COL_GLA_K // 512)),
            pl.BlockSpec((tm, HALF), lambda i: (i, COL_GLA_V // HALF)),
            pl.BlockSpec((tm, HALF), lambda i: (i, COL_GLA_V // HALF + 1)),
            pl.BlockSpec((tm, HALF), lambda i: (i, COL_GLA_G // HALF)),
            pl.BlockSpec((tm, HALF), lambda i: (i, COL_GLA_G // HALF + 1)),
            pl.BlockSpec((tm, LANES), lambda i: (i, COL_LR // LANES)),
            pl.BlockSpec((LANES, 512), lambda i: (0, 0)),
            pl.BlockSpec((1, 512), lambda i: (0, 0)),
            pl.BlockSpec((1, GLA_DV), lambda i: (0, 0)),
        ],
        out_specs=pl.BlockSpec((tm, 1024), lambda i: (i, 0)),
        out_shape=jax.ShapeDtypeStruct((nt, 1024), jnp.bfloat16),
        scratch_shapes=[
            pltpu.VMEM((GLA_HEADS, GLA_DV, GLA_DK), jnp.float32),
            pltpu.VMEM((tm, tm), jnp.bfloat16),
            pltpu.VMEM((tm, GLA_HEADS * GLA_DK), jnp.bfloat16),
            pltpu.VMEM((tm, GLA_HEADS * GLA_DK), jnp.bfloat16),
            pltpu.VMEM((tm, GLA_HEADS * GLA_DK), jnp.float32),
            pltpu.VMEM((tm, GLA_HEADS * GLA_DV), jnp.float32),
            pltpu.VMEM((tm // GLA_CHUNK, GLA_HEADS, GLA_DV, GLA_DK), jnp.float32),
        ],
        compiler_params=_params("arbitrary"),
        name="gla",
    )(*([proj] * 7), w2, gate_b, norm_g)


def _merge_kernel(x_ref, ya_ref, yb_ref, yc_ref, ga_ref, gb_ref, gc_ref, wa_ref, wb_ref, wc_ref, wo_ref, g_ref, o_ref):
    def branch(gate_ref, y_ref, w_ref):
        return jax.nn.sigmoid(gate_ref[...].astype(jnp.float32)) * _dot(y_ref[...], w_ref[...])

    merged = branch(ga_ref, ya_ref, wa_ref) + branch(gb_ref, yb_ref, wb_ref) + branch(gc_ref, yc_ref, wc_ref)
    m = _dot(merged.astype(jnp.bfloat16), wo_ref[...])
    o_ref[...] = x_ref[...] + _rms(m, g_ref[...])


def _merge(x, ya, yb, yc, proj, wa, wb, wc, wo, g, l):
    nt, d = x.shape
    bm = min(MERGE_BLOCK_M, nt)
    c0 = COL_GATES // d

    def resident(shape):
        return pl.BlockSpec((None,) + shape, lambda i: (l, 0, 0), pipeline_mode=pl.Buffered(1))

    def branch_in():
        return pl.BlockSpec((bm, 1024), lambda i: (i, 0))

    return pl.pallas_call(
        _merge_kernel,
        grid=(nt // bm,),
        in_specs=[
            pl.BlockSpec((bm, d), lambda i: (i, 0)),
            branch_in(), branch_in(), branch_in(),
            pl.BlockSpec((bm, d), lambda i: (i, c0)),
            pl.BlockSpec((bm, d), lambda i: (i, c0 + 1)),
            pl.BlockSpec((bm, d), lambda i: (i, c0 + 2)),
            resident((1024, d)), resident((1024, d)), resident((1024, d)), resident((d, d)),
            pl.BlockSpec((1, d), lambda i: (0, 0)),
        ],
        out_specs=pl.BlockSpec((bm, d), lambda i: (i, 0)),
        out_shape=jax.ShapeDtypeStruct((nt, d), jnp.float32),
        compiler_params=_params("parallel"),
        name="merge",
    )(x, ya, yb, yc, proj, proj, proj, wa, wb, wc, wo, g)


def _split_w_in(w_in):
    gates = w_in[..., MAIN_COLS:].astype(jnp.bfloat16)
    main = jnp.pad(w_in[..., :MAIN_COLS].astype(jnp.bfloat16), ((0, 0), (0, 0), (0, MAIN_WIDTH - MAIN_COLS)))
    return gates, main


def kernel(x, rel_bias, ffn1_pre_g, ffn1_post_g, ffn1_w_gate, ffn1_w_up, ffn1_w_down, mix_pre_g, mix_post_g, w_in, attn_sink, conv_w, conv_b, conv_ln_g, conv_ln_b, gla_gate_w2, gla_gate_b, gla_norm_g, w_a_up, w_b_up, w_c_up, w_out, ffn2_pre_g, ffn2_post_g, ffn2_w_gate, ffn2_w_up, ffn2_w_down):
    batch, seq, d = x.shape
    depth = w_in.shape[0]
    bf = lambda w: w.astype(jnp.bfloat16)
    row = lambda v: v.reshape(1, -1)
    bias = _band_bias(rel_bias)
    f1g, f1u, f1d = bf(ffn1_w_gate), bf(ffn1_w_up), bf(ffn1_w_down)
    f2g, f2u, f2d = bf(ffn2_w_gate), bf(ffn2_w_up), bf(ffn2_w_down)
    w_gates, w_main = _split_w_in(w_in)
    wa, wb, wc, wo = bf(w_a_up), bf(w_b_up), bf(w_c_up), bf(w_out)
    w2 = jnp.pad(bf(gla_gate_w2), ((0, 0), (0, LANES - GLA_RANK), (0, 0)))
    xt = x.reshape(batch * seq, d)
    for l in range(depth):
        xt = _ffn(xt, row(ffn1_pre_g[l]), row(ffn1_post_g[l]), f1g, f1u, f1d, l)
        proj = _proj(xt, row(mix_pre_g[l]), w_gates, w_main, l)
        ya = _attention(proj, attn_sink[l], bias, seq)
        yb = _conv(proj, conv_w[l], row(conv_b[l]), row(conv_ln_g[l]), row(conv_ln_b[l]), seq)
        yc = _gla(proj, w2[l], row(gla_gate_b[l]), row(gla_norm_g[l]), seq)
        xt = _merge(xt, ya, yb, yc, proj, wa, wb, wc, wo, row(mix_post_g[l]), l)
        xt = _ffn(xt, row(ffn2_pre_g[l]), row(ffn2_post_g[l]), f2g, f2u, f2d, l)
    return xt.reshape(batch, seq, d)
```

```python
import functools
import math

import jax
import jax.numpy as jnp
from jax import lax
from jax.experimental import pallas as pl
from jax.experimental.pallas import tpu as pltpu

EPS = 1e-6
D_MODEL = 2048
D_FF = 5632
HEAD_DIM = 64
N_HEADS = 16
N_KV_HEADS = 4
ATTN_BLOCK = 128
REL_BUCKETS = 32
REL_MAX_DIST = 128
CONV_CH = 1024
CONV_WIDTH = 31
GLA_HEADS = 4
GLA_DK = 128
GLA_DV = 256
GLA_CHUNK = 64
GLA_RANK = 16
GLA_GATE_NORM = 16.0
MASK_VALUE = -1e30

LANES = 128
SUBLANES = 8
VMEM_LIMIT_BYTES = 56 * 1024 * 1024

COL_GATES = 0
COL_MAIN = 3 * D_MODEL
COL_Q = COL_MAIN
COL_K = COL_MAIN + 1024
COL_V = COL_MAIN + 1280
COL_CONV_A = COL_MAIN + 1536
COL_CONV_G = COL_MAIN + 2560
COL_GLA_Q = COL_MAIN + 3584
COL_GLA_K = COL_MAIN + 4096
COL_GLA_V = COL_MAIN + 4608
COL_GLA_G = COL_MAIN + 5632
COL_LR = COL_MAIN + 6656
MAIN_COLS = 6672
MAIN_WIDTH = 7168
HALF = 512
PROJ_BLOCK_N = 1024

FFN_BLOCK_M = 1024
FFN_BLOCK_F = 512
FFN_SPLIT = 2
FFN_SLICES = 8
FFN_SLICE_ALIGN = 16
PROJ_BLOCK_M = 1024
ATTN_TOKENS = 1024
CONV_TOKENS = 512
CONV_HALO = 32
CONV_ROWS = 32
CONV_COLS = 512
GLA_TOKENS = 512
GLA_UNROLL = 2
MERGE_BLOCK_M = 256


def _params(*semantics):
    return pltpu.CompilerParams(dimension_semantics=semantics, vmem_limit_bytes=VMEM_LIMIT_BYTES)


def _rms(x, g):
    return x * lax.rsqrt(jnp.mean(x * x, axis=-1, keepdims=True) + EPS) * g


def _dot(a, b):
    return jnp.dot(a, b, preferred_element_type=jnp.float32)


def _dot_nt(a, b):
    return lax.dot_general(a, b, (((1,), (1,)), ((), ())), preferred_element_type=jnp.float32)


def _dot_tn(a, b):
    return lax.dot_general(a, b, (((0,), (0,)), ((), ())), preferred_element_type=jnp.float32)


def _ffn_kernel(xp_ref, xn_ref, gpre_ref, gpost_ref, wg_ref, wu_ref, wd_ref, o_ref, h_scr, acc_scr, *, nb):
    r, f = pl.program_id(0), pl.program_id(1)
    rows = pl.ds(pl.multiple_of(jnp.minimum(f, FFN_SLICES - 1) * xn_ref.shape[0], xn_ref.shape[0]), xn_ref.shape[0])

    @pl.when((r == 0) & (f == 0))
    def _():
        acc_scr[...] = jnp.zeros_like(acc_scr)

    def finish(slot):
        o_ref[...] = xp_ref[...] + 0.5 * _rms(acc_scr[slot, rows, :], gpost_ref[...])

    def prepare(slot):
        h_scr[slot, rows, :] = _rms(xn_ref[...], gpre_ref[...]).astype(jnp.bfloat16)

    def matmuls(slot):
        h = h_scr[slot]
        acts = []
        for s in range(FFN_SPLIT):
            cols = slice(s * wg_ref.shape[1] // FFN_SPLIT, (s + 1) * wg_ref.shape[1] // FFN_SPLIT)
            a = _dot(h, wg_ref[:, cols])
            u = _dot(h, wu_ref[:, cols])
            acts.append((a * jax.nn.sigmoid(a) * u).astype(jnp.bfloat16))
        carried = jnp.where(f > 0, acc_scr[slot], 0.0)
        acc_scr[slot] = carried + _dot(jnp.concatenate(acts, axis=1), wd_ref[...])

    @pl.when(r == 0)
    def _():
        prepare(0)

    for parity in (0, 1):
        @pl.when((r >= 1) & (r <= nb) & ((r - 1) % 2 == parity))
        def _():
            finish(1 - parity)
            matmuls(parity)
            prepare(1 - parity)

    @pl.when(r == nb + 1)
    def _():
        finish((nb - 1) % 2)


def _ffn(x, gpre, gpost, wg, wu, wd, l):
    nt, d = x.shape
    ff = wg.shape[2]
    bm = min(FFN_BLOCK_M, nt)
    bf = FFN_BLOCK_F
    nb, nf = nt // bm, ff // bf
    rs = bm // FFN_SLICES
    assert nf >= FFN_SLICES and rs % FFN_SLICE_ALIGN == 0

    def row_slice(block, f):
        return jnp.clip(block, 0, nb - 1) * FFN_SLICES + jnp.minimum(f, FFN_SLICES - 1)

    def chunk(r, f):
        return jnp.where(r == 0, 0, jnp.where(r == nb + 1, nf - 1, f))

    return pl.pallas_call(
        functools.partial(_ffn_kernel, nb=nb),
        grid=(nb + 2, nf),
        in_specs=[
            pl.BlockSpec((rs, d), lambda r, f: (row_slice(r - 2, f), 0)),
            pl.BlockSpec((rs, d), lambda r, f: (row_slice(r, f), 0)),
            pl.BlockSpec((1, d), lambda r, f: (0, 0)),
            pl.BlockSpec((1, d), lambda r, f: (0, 0)),
            pl.BlockSpec((None, d, bf), lambda r, f: (l, 0, chunk(r, f))),
            pl.BlockSpec((None, d, bf), lambda r, f: (l, 0, chunk(r, f))),
            pl.BlockSpec((None, bf, d), lambda r, f: (l, chunk(r, f), 0)),
        ],
        out_specs=pl.BlockSpec((rs, d), lambda r, f: (jnp.where(r < 2, 0, row_slice(r - 2, f)), 0)),
        out_shape=jax.ShapeDtypeStruct((nt, d), jnp.float32),
        scratch_shapes=[pltpu.VMEM((2, bm, d), jnp.bfloat16), pltpu.VMEM((2, bm, d), jnp.float32)],
        compiler_params=_params("arbitrary", "arbitrary"),
        name="ffn",
    )(x, x, gpre, gpost, wg, wu, wd)


def _proj_kernel(x_ref, g_ref, wg_ref, wm_ref, o_ref, h_scr, *, gate_blocks):
    j = pl.program_id(1)

    @pl.when(j == 0)
    def _():
        h_scr[...] = _rms(x_ref[...], g_ref[...]).astype(jnp.bfloat16)

    @pl.when(j < gate_blocks)
    def _():
        o_ref[...] = _dot(h_scr[...], wg_ref[...]).astype(o_ref.dtype)

    @pl.when(j >= gate_blocks)
    def _():
        o_ref[...] = _dot(h_scr[...], wm_ref[...]).astype(o_ref.dtype)


def _proj(x, g, w_gates, w_main, l):
    nt, d = x.shape
    bm = min(PROJ_BLOCK_M, nt)
    bn = PROJ_BLOCK_N
    gate_blocks, main_blocks = w_gates.shape[2] // bn, w_main.shape[2] // bn
    return pl.pallas_call(
        functools.partial(_proj_kernel, gate_blocks=gate_blocks),
        grid=(nt // bm, gate_blocks + main_blocks),
        in_specs=[
            pl.BlockSpec((bm, d), lambda i, j: (i, 0)),
            pl.BlockSpec((1, d), lambda i, j: (0, 0)),
            pl.BlockSpec((None, d, bn), lambda i, j: (l, 0, jnp.minimum(j, gate_blocks - 1))),
            pl.BlockSpec((None, d, bn), lambda i, j: (
                l, 0, jnp.where(j < gate_blocks - 1, main_blocks - 1, jnp.maximum(j - gate_blocks, 0)))),
        ],
        out_specs=pl.BlockSpec((bm, bn), lambda i, j: (i, j)),
        out_shape=jax.ShapeDtypeStruct((nt, (gate_blocks + main_blocks) * bn), jnp.bfloat16),
        scratch_shapes=[pltpu.VMEM((bm, d), jnp.bfloat16)],
        compiler_params=_params("parallel", "arbitrary"),
        name="proj",
    )(x, g, w_gates, w_main)


def _rel_buckets():
    qi = jnp.arange(ATTN_BLOCK)[None, :]
    kj = jnp.arange(2 * ATTN_BLOCK)[:, None]
    n = jnp.maximum(ATTN_BLOCK + qi - kj, 0)
    max_exact = REL_BUCKETS // 2
    nf = jnp.maximum(n, 1).astype(jnp.float32)
    large = max_exact + (jnp.log(nf / max_exact) / math.log(REL_MAX_DIST / max_exact)
                         * (REL_BUCKETS - max_exact)).astype(jnp.int32)
    large = jnp.minimum(large, REL_BUCKETS - 1)
    return jnp.where(n < max_exact, n, large).astype(jnp.int32)


def _bias_kernel(rel_ref, bucket_ref, o_ref):
    h = pl.program_id(0)
    bucket = bucket_ref[...]
    kj = lax.broadcasted_iota(jnp.int32, bucket.shape, 0)
    qi = lax.broadcasted_iota(jnp.int32, bucket.shape, 1)
    dist = ATTN_BLOCK + qi - kj
    acc = jnp.zeros(bucket.shape, jnp.float32)
    for b in range(REL_BUCKETS):
        acc = jnp.where(bucket == b, rel_ref[b, h], acc)
    o_ref[0] = jnp.where((dist >= 0) & (dist < ATTN_BLOCK), acc, MASK_VALUE)


def _band_bias(rel_bias):
    shape = (2 * ATTN_BLOCK, ATTN_BLOCK)
    return pl.pallas_call(
        _bias_kernel,
        grid=(N_HEADS,),
        in_specs=[
            pl.BlockSpec(memory_space=pltpu.SMEM),
            pl.BlockSpec(shape, lambda h: (0, 0)),
        ],
        out_specs=pl.BlockSpec((1,) + shape, lambda h: (h, 0, 0)),
        out_shape=jax.ShapeDtypeStruct((N_HEADS,) + shape, jnp.float32),
        compiler_params=_params("arbitrary"),
        name="band_bias",
    )(rel_bias, _rel_buckets())


def _lane_swap_halves(x):
    y = pltpu.roll(pltpu.bitcast(x, jnp.uint32), HEAD_DIM, axis=1)
    return pltpu.bitcast(y, jnp.bfloat16)


def _head_pair_tiles(t128, head_in_high_lanes):
    lane = lax.broadcasted_iota(jnp.int32, t128.shape, 1)
    zero = jnp.zeros_like(t128)
    if head_in_high_lanes:
        hi = jnp.where(lane >= HEAD_DIM, t128, zero)
        return _lane_swap_halves(hi), hi
    lo = jnp.where(lane < HEAD_DIM, t128, zero)
    return lo, _lane_swap_halves(lo)


def _attn_kernel(sink_ref, q_ref, k_ref, v_ref, kp_ref, vp_ref, bias_ref, o_ref, kfull, vfull, s_even, s_odd, p_even, p_odd,
                 *, steps_per_seq):
    blk = ATTN_BLOCK
    nblk = q_ref.shape[0] // blk
    first_step = pl.program_id(0) % steps_per_seq == 0
    kfull[0:blk, :] = kp_ref[...]
    vfull[0:blk, :] = vp_ref[...]
    kfull[blk:, :] = k_ref[...]
    vfull[blk:, :] = v_ref[...]

    def first_row(n):
        return n * blk if isinstance(n, int) else pl.multiple_of(n * blk, blk)

    def scores(n, s_scr):
        r0 = first_row(n)
        kband = kfull[pl.ds(r0, 2 * blk), :]
        for j in range(N_KV_HEADS):
            c = (j // 2) * LANES
            ka, kb = _head_pair_tiles(kband[:, c:c + LANES], j % 2 == 1)
            q0 = q_ref[pl.ds(r0, blk), j * 256:j * 256 + LANES]
            q1 = q_ref[pl.ds(r0, blk), j * 256 + LANES:(j + 1) * 256]
            qs = jnp.concatenate([q0, q1], axis=0) * (HEAD_DIM ** -0.5)
            s_scr[j] = _dot_nt(jnp.concatenate([ka, kb], axis=0), qs)

    def softmax(s_scr, p_scr, maybe_no_prev=False):
        for h in range(N_HEADS):
            j, g = h // 4, h % 4
            logits = s_scr[j, (g % 2) * 2 * blk:(g % 2 + 1) * 2 * blk, (g // 2) * blk:(g // 2 + 1) * blk]
            logits = logits + bias_ref[h]
            if maybe_no_prev:
                prev_rows = jnp.where(first_step, MASK_VALUE, logits[:blk])
                logits = jnp.concatenate([prev_rows, logits[blk:]], axis=0)
            sink = sink_ref[h]
            m = jnp.maximum(jnp.max(logits, axis=0, keepdims=True), sink)
            p = jnp.exp(logits - m)
            key0 = lax.broadcasted_iota(jnp.int32, (SUBLANES, blk), 0) == 0
            top = jnp.where(key0, jnp.exp(sink - m), p[:SUBLANES])
            p_scr[h] = jnp.concatenate([top, p[SUBLANES:]], axis=0).astype(jnp.bfloat16)

    lane = lax.broadcasted_iota(jnp.int32, (2 * blk, LANES), 1)
    ones_even = (lane < HEAD_DIM).astype(jnp.bfloat16)
    ones_odd = (lane >= HEAD_DIM).astype(jnp.bfloat16)
    not_key0 = lax.broadcasted_iota(jnp.int32, (2 * blk, LANES), 0) > 0

    def values(n, p_scr):
        r0 = first_row(n)
        vband = vfull[pl.ds(r0, 2 * blk), :]
        for j in range(N_KV_HEADS):
            c = (j // 2) * LANES
            v_tile = jnp.where(not_key0, vband[:, c:c + LANES], jnp.zeros((2 * blk, LANES), jnp.bfloat16))
            va, vb = _head_pair_tiles(v_tile, j % 2 == 1)
            rhs_even = jnp.concatenate([va, ones_even], axis=1)
            rhs_odd = jnp.concatenate([vb, ones_odd], axis=1)
            for pair in range(2):
                h = j * 4 + 2 * pair
                out = _dot_tn(p_scr[h], rhs_even) + _dot_tn(p_scr[h + 1], rhs_odd)
                out = out[:, :LANES] * (1.0 / out[:, LANES:])
                col = j * 256 + pair * LANES
                o_ref[pl.ds(r0, blk), col:col + LANES] = out.astype(o_ref.dtype)

    def even_block(t, has_prev=True, maybe_no_prev=False):
        scores(t + 1, s_odd)
        softmax(s_even, p_even, maybe_no_prev)
        if has_prev:
            values(t - 1, p_odd)

    def odd_block(t, has_next=True):
        if has_next:
            scores(t + 1, s_even)
        softmax(s_odd, p_odd)
        values(t - 1, p_even)

    def block_pair(u, carry):
        t = 2 * u
        even_block(t)
        odd_block(t + 1)
        return carry

    scores(0, s_even)
    even_block(0, has_prev=False, maybe_no_prev=True)
    odd_block(1)
    lax.fori_loop(1, nblk // 2 - 1, block_pair, 0)
    even_block(nblk - 2)
    odd_block(nblk - 1, has_next=False)
    values(nblk - 1, p_odd)


def _attention(proj, sink, bias, seq):
    nt = proj.shape[0]
    tq = min(ATTN_TOKENS, seq)
    per = tq // ATTN_BLOCK
    assert per % 2 == 0 and per >= 4, "the block pipeline needs an even number (>= 4) of blocks per step"
    steps_per_seq = seq // tq
    cq, ck, cv = COL_Q // 1024, COL_K // 256, COL_V // 256

    def prev(i):
        return jnp.maximum(i * per - 1, 0)

    return pl.pallas_call(
        functools.partial(_attn_kernel, steps_per_seq=steps_per_seq),
        grid=(nt // tq,),
        in_specs=[
            pl.BlockSpec(memory_space=pltpu.SMEM),
            pl.BlockSpec((tq, 1024), lambda i: (i, cq)),
            pl.BlockSpec((tq, 256), lambda i: (i, ck)),
            pl.BlockSpec((tq, 256), lambda i: (i, cv)),
            pl.BlockSpec((ATTN_BLOCK, 256), lambda i: (prev(i), ck)),
            pl.BlockSpec((ATTN_BLOCK, 256), lambda i: (prev(i), cv)),
            pl.BlockSpec((N_HEADS, 2 * ATTN_BLOCK, ATTN_BLOCK), lambda i: (0, 0, 0)),
        ],
        out_specs=pl.BlockSpec((tq, 1024), lambda i: (i, 0)),
        out_shape=jax.ShapeDtypeStruct((nt, 1024), jnp.bfloat16),
        scratch_shapes=[pltpu.VMEM((tq + ATTN_BLOCK, 256), jnp.bfloat16),
                        pltpu.VMEM((tq + ATTN_BLOCK, 256), jnp.bfloat16),
                        pltpu.VMEM((N_KV_HEADS, 4 * ATTN_BLOCK, 2 * ATTN_BLOCK), jnp.float32),
                        pltpu.VMEM((N_KV_HEADS, 4 * ATTN_BLOCK, 2 * ATTN_BLOCK), jnp.float32),
                        pltpu.VMEM((N_HEADS, 2 * ATTN_BLOCK, ATTN_BLOCK), jnp.bfloat16),
                        pltpu.VMEM((N_HEADS, 2 * ATTN_BLOCK, ATTN_BLOCK), jnp.bfloat16)],
        compiler_params=_params("arbitrary"),
        name="attention",
    )(sink, proj, proj, proj, proj, proj, bias)


def _conv_kernel(a0_ref, a1_ref, g0_ref, g1_ref, ap0_ref, ap1_ref, gp0_ref, gp1_ref, w_ref, b_ref, lng_ref, lnb_ref,
                 o_ref, y_scr, c_scr, wb_scr, *, steps_per_seq):
    tm = a0_ref.shape[0]
    halo = CONV_HALO
    first_step = pl.program_id(0) % steps_per_seq == 0
    halves = ((a0_ref, g0_ref, ap0_ref, gp0_ref), (a1_ref, g1_ref, ap1_ref, gp1_ref))
    for half, (a_ref, g_ref, ap_ref, gp_ref) in enumerate(halves):
        hs = slice(half * HALF, (half + 1) * HALF)
        yp = ap_ref[...].astype(jnp.float32) * jax.nn.sigmoid(gp_ref[...].astype(jnp.float32))
        y_scr[0, 0:halo, hs] = jnp.where(first_step, 0.0, yp)
        y_scr[0, halo:, hs] = a_ref[...].astype(jnp.float32) * jax.nn.sigmoid(g_ref[...].astype(jnp.float32))
    shifted_rows = tm + halo - SUBLANES
    for r in range(1, SUBLANES):
        y_scr[r, 0:shifted_rows, :] = y_scr[0, r:r + shifted_rows, :]
    base = halo - (CONV_WIDTH - 1)

    @pl.when(pl.program_id(0) == 0)
    def _():
        for j in range(CONV_WIDTH):
            wb_scr[j] = jnp.broadcast_to(w_ref[j:j + 1, :], (SUBLANES, CONV_CH))
        wb_scr[CONV_WIDTH] = jnp.broadcast_to(b_ref[...], (SUBLANES, CONV_CH))

    lane_chunks = CONV_CH // CONV_COLS
    row_groups = CONV_ROWS // SUBLANES

    def tile(t, carry):
        r0 = pl.multiple_of((t // lane_chunks) * CONV_ROWS, CONV_ROWS)
        cs = pl.ds(pl.multiple_of((t % lane_chunks) * CONV_COLS, CONV_COLS), CONV_COLS)
        accs = [wb_scr[CONV_WIDTH, :, cs]] * row_groups
        for phase in range(SUBLANES):
            taps = [j for j in range(CONV_WIDTH) if (base + j) % SUBLANES == phase]
            first = [(base + j) // SUBLANES for j in taps]
            groups = {k: y_scr[phase, pl.ds(r0 + k * SUBLANES, SUBLANES), cs]
                      for k in range(min(first), max(first) + row_groups)}
            for j, k0 in zip(taps, first):
                wj = wb_scr[j, :, cs]
                for g in range(row_groups):
                    accs[g] = accs[g] + groups[k0 + g] * wj
        for g in range(row_groups):
            c_scr[pl.ds(r0 + g * SUBLANES, SUBLANES), cs] = accs[g]
        return carry

    lax.fori_loop(0, (tm // CONV_ROWS) * lane_chunks, tile, 0)
    y = c_scr[...]
    mu = jnp.mean(y, axis=-1, keepdims=True)
    yc = y - mu
    z = yc * lax.rsqrt(jnp.mean(yc * yc, axis=-1, keepdims=True) + EPS) * lng_ref[...] + lnb_ref[...]
    o_ref[...] = (z * jax.nn.sigmoid(z)).astype(o_ref.dtype)


def _conv(proj, w, b, ln_g, ln_b, seq):
    nt = proj.shape[0]
    tm = min(CONV_TOKENS, seq)
    steps_per_seq = seq // tm
    per = tm // CONV_HALO
    ca, cg = COL_CONV_A // HALF, COL_CONV_G // HALF

    def prev(i):
        return jnp.maximum(i * per - 1, 0)

    def cur(c):
        return pl.BlockSpec((tm, HALF), lambda i: (i, c))

    def before(c):
        return pl.BlockSpec((CONV_HALO, HALF), lambda i: (prev(i), c))

    vec = pl.BlockSpec((1, CONV_CH), lambda i: (0, 0))
    return pl.pallas_call(
        functools.partial(_conv_kernel, steps_per_seq=steps_per_seq),
        grid=(nt // tm,),
        in_specs=[
            cur(ca), cur(ca + 1), cur(cg), cur(cg + 1),
            before(ca), before(ca + 1), before(cg), before(cg + 1),
            pl.BlockSpec((CONV_WIDTH, CONV_CH), lambda i: (0, 0)),
            vec, vec, vec,
        ],
        out_specs=pl.BlockSpec((tm, CONV_CH), lambda i: (i, 0)),
        out_shape=jax.ShapeDtypeStruct((nt, CONV_CH), jnp.bfloat16),
        scratch_shapes=[pltpu.VMEM((SUBLANES, tm + CONV_HALO, CONV_CH), jnp.float32),
                        pltpu.VMEM((tm, CONV_CH), jnp.float32),
                        pltpu.VMEM((CONV_WIDTH + 1, SUBLANES, CONV_CH), jnp.float32)],
        compiler_params=_params("arbitrary"),
        name="conformer_conv",
    )(*([proj] * 8), w, b, ln_g, ln_b)


def _split_bf16(x):
    hi = x.astype(jnp.bfloat16)
    lo = (x - hi.astype(jnp.float32)).astype(jnp.bfloat16)
    return hi, lo


def _gla_kernel(q_ref, k_ref, v0_ref, v1_ref, g0_ref, g1_ref, lr_ref, w2_ref, gb_ref, ng_ref, o_ref,
                state_t, tri_scr, qe_scr, ke_scr, b_scr, oi_scr, kvt_scr, *, steps_per_seq):
    L = GLA_CHUNK
    tm = q_ref.shape[0]
    nchunk = tm // L

    @pl.when(pl.program_id(0) == 0)
    def _():
        row = lax.broadcasted_iota(jnp.int32, (tm, tm), 0)
        col = lax.broadcasted_iota(jnp.int32, (tm, tm), 1)
        tri_scr[...] = ((row // L == col // L) & (row >= col)).astype(jnp.bfloat16)

    @pl.when(pl.program_id(0) % steps_per_seq == 0)
    def _():
        state_t[...] = jnp.zeros_like(state_t)

    z = _dot(lr_ref[...], w2_ref[...]) + gb_ref[...]
    gk = (jnp.minimum(z, 0.0) - jnp.log(1.0 + jnp.exp(-jnp.abs(z)))) * (1.0 / GLA_GATE_NORM)
    gk_hi, gk_lo = _split_bf16(gk)
    b = _dot(tri_scr[...], gk_hi) + _dot(tri_scr[...], gk_lo)
    b_scr[...] = b
    qe_scr[...] = (q_ref[...].astype(jnp.float32) * (GLA_DK ** -0.5) * jnp.exp(b)).astype(jnp.bfloat16)
    ke_scr[...] = (k_ref[...].astype(jnp.float32) * jnp.exp(-b)).astype(jnp.bfloat16)

    causal = lax.broadcasted_iota(jnp.int32, (L, L), 0) >= lax.broadcasted_iota(jnp.int32, (L, L), 1)

    def head_cols(halves, rs, h):
        per_half = HALF // GLA_DV
        return halves[h // per_half][rs, (h % per_half) * GLA_DV:(h % per_half + 1) * GLA_DV]

    def local(c, carry):
        rs = pl.ds(pl.multiple_of(c * L, L), L)
        heads = [(slice(h * GLA_DK, (h + 1) * GLA_DK), slice(h * GLA_DV, (h + 1) * GLA_DV)) for h in range(GLA_HEADS)]
        scores = [_dot_nt(qe_scr[rs, ks], ke_scr[rs, ks]) for ks, _ in heads]
        for h, (ks, vs) in enumerate(heads):
            kvt_scr[c, h] = _dot_tn(head_cols((v0_ref, v1_ref), rs, h), ke_scr[rs, ks])
        atts = [jnp.where(causal, s, 0.0).astype(jnp.bfloat16) for s in scores]
        for h, (_, vs) in enumerate(heads):
            oi_scr[rs, vs] = _dot(atts[h], head_cols((v0_ref, v1_ref), rs, h))
        return carry

    lax.fori_loop(0, nchunk, local, 0, unroll=GLA_UNROLL)

    def recur(c, carry):
        r0 = pl.multiple_of(c * L, L)
        rs = pl.ds(r0, L)
        b_tail = b_scr[pl.ds(r0 + L - SUBLANES, SUBLANES), :]
        for h in range(GLA_HEADS):
            ks = slice(h * GLA_DK, (h + 1) * GLA_DK)
            vs = slice(h * GLA_DV, (h + 1) * GLA_DV)
            s_old = state_t[h]
            o = oi_scr[rs, vs] + _dot_nt(qe_scr[rs, ks], s_old.astype(jnp.bfloat16))
            decay = jnp.exp(b_tail[SUBLANES - 1:SUBLANES, ks])
            state_t[h] = decay * (s_old + kvt_scr[c, h])
            o = _rms(o, ng_ref[...])
            gate = head_cols((g0_ref, g1_ref), rs, h).astype(jnp.float32)
            o_ref[rs, vs] = (o * (gate * jax.nn.sigmoid(gate))).astype(o_ref.dtype)
        return carry

    lax.fori_loop(0, nchunk, recur, 0, unroll=GLA_UNROLL)


def _gla(proj, w2, gate_b, norm_g, seq):
    nt = proj.shape[0]
    tm = min(GLA_TOKENS, seq)
    steps_per_seq = seq // tm
    return pl.pallas_call(
        functools.partial(_gla_kernel, steps_per_seq=steps_per_seq),
        grid=(nt // tm,),
        in_specs=[
            pl.BlockSpec((tm, 512), lambda i: (i, COL_GLA_Q // 512)),
            pl.BlockSpec((tm, 512), lambda i: (i, COL_GLA_K // 512)),
            pl.BlockSpec((tm, HALF), lambda i: (i, COL_GLA_V // HALF)),
            pl.BlockSpec((tm, HALF), lambda i: (i, COL_GLA_V // HALF + 1)),
            pl.BlockSpec((tm, HALF), lambda i: (i, COL_GLA_G // HALF)),
            pl.BlockSpec((tm, HALF), lambda i: (i, COL_GLA_G // HALF + 1)),
            pl.BlockSpec((tm, LANES), lambda i: (i, COL_LR // LANES)),
            pl.BlockSpec((LANES, 512), lambda i: (0, 0)),
            pl.BlockSpec((1, 512), lambda i: (0, 0)),
            pl.BlockSpec((1, GLA_DV), lambda i: (0, 0)),
        ],
        out_specs=pl.BlockSpec((tm, 1024), lambda i: (i, 0)),
        out_shape=jax.ShapeDtypeStruct((nt, 1024), jnp.bfloat16),
        scratch_shapes=[
            pltpu.VMEM((GLA_HEADS, GLA_DV, GLA_DK), jnp.float32),
            pltpu.VMEM((tm, tm), jnp.bfloat16),
            pltpu.VMEM((tm, GLA_HEADS * GLA_DK), jnp.bfloat16),
            pltpu.VMEM((tm, GLA_HEADS * GLA_DK), jnp.bfloat16),
            pltpu.VMEM((tm, GLA_HEADS * GLA_DK), jnp.float32),
            pltpu.VMEM((tm, GLA_HEADS * GLA_DV), jnp.float32),
            pltpu.VMEM((tm // GLA_CHUNK, GLA_HEADS, GLA_DV, GLA_DK), jnp.float32),
        ],
        compiler_params=_params("arbitrary"),
        name="gla",
    )(*([proj] * 7), w2, gate_b, norm_g)


def _merge_kernel(x_ref, ya_ref, yb_ref, yc_ref, ga_ref, gb_ref, gc_ref, wa_ref, wb_ref, wc_ref, wo_ref, g_ref, o_ref):
    def branch(gate_ref, y_ref, w_ref):
        return jax.nn.sigmoid(gate_ref[...].astype(jnp.float32)) * _dot(y_ref[...], w_ref[...])

    merged = branch(ga_ref, ya_ref, wa_ref) + branch(gb_ref, yb_ref, wb_ref) + branch(gc_ref, yc_ref, wc_ref)
    m = _dot(merged.astype(jnp.bfloat16), wo_ref[...])
    o_ref[...] = x_ref[...] + _rms(m, g_ref[...])


def _merge(x, ya, yb, yc, proj, wa, wb, wc, wo, g, l):
    nt, d = x.shape
    bm = min(MERGE_BLOCK_M, nt)
    c0 = COL_GATES // d

    def resident(shape):
        return pl.BlockSpec((None,) + shape, lambda i: (l, 0, 0), pipeline_mode=pl.Buffered(1))

    def branch_in():
        return pl.BlockSpec((bm, 1024), lambda i: (i, 0))

    return pl.pallas_call(
        _merge_kernel,
        grid=(nt // bm,),
        in_specs=[
            pl.BlockSpec((bm, d), lambda i: (i, 0)),
            branch_in(), branch_in(), branch_in(),
            pl.BlockSpec((bm, d), lambda i: (i, c0)),
            pl.BlockSpec((bm, d), lambda i: (i, c0 + 1)),
            pl.BlockSpec((bm, d), lambda i: (i, c0 + 2)),
            resident((1024, d)), resident((1024, d)), resident((1024, d)), resident((d, d)),
            pl.BlockSpec((1, d), lambda i: (0, 0)),
        ],
        out_specs=pl.BlockSpec((bm, d), lambda i: (i, 0)),
        out_shape=jax.ShapeDtypeStruct((nt, d), jnp.float32),
        compiler_params=_params("parallel"),
        name="merge",
    )(x, ya, yb, yc, proj, proj, proj, wa, wb, wc, wo, g)


def _split_w_in(w_in):
    gates = w_in[..., MAIN_COLS:].astype(jnp.bfloat16)
    main = jnp.pad(w_in[..., :MAIN_COLS].astype(jnp.bfloat16), ((0, 0), (0, 0), (0, MAIN_WIDTH - MAIN_COLS)))
    return gates, main


def kernel(x, rel_bias, ffn1_pre_g, ffn1_post_g, ffn1_w_gate, ffn1_w_up, ffn1_w_down, mix_pre_g, mix_post_g, w_in, attn_sink, conv_w, conv_b, conv_ln_g, conv_ln_b, gla_gate_w2, gla_gate_b, gla_norm_g, w_a_up, w_b_up, w_c_up, w_out, ffn2_pre_g, ffn2_post_g, ffn2_w_gate, ffn2_w_up, ffn2_w_down):
    batch, seq, d = x.shape
    depth = w_in.shape[0]
    bf = lambda w: w.astype(jnp.bfloat16)
    row = lambda v: v.reshape(1, -1)
    bias = _band_bias(rel_bias)
    f1g, f1u, f1d = bf(ffn1_w_gate), bf(ffn1_w_up), bf(ffn1_w_down)
    f2g, f2u, f2d = bf(ffn2_w_gate), bf(ffn2_w_up), bf(ffn2_w_down)
    w_gates, w_main = _split_w_in(w_in)
    wa, wb, wc, wo = bf(w_a_up), bf(w_b_up), bf(w_c_up), bf(w_out)
    w2 = jnp.pad(bf(gla_gate_w2), ((0, 0), (0, LANES - GLA_RANK), (0, 0)))
    xt = x.reshape(batch * seq, d)
    for l in range(depth):
        xt = _ffn(xt, row(ffn1_pre_g[l]), row(ffn1_post_g[l]), f1g, f1u, f1d, l)
        proj = _proj(xt, row(mix_pre_g[l]), w_gates, w_main, l)
        ya = _attention(proj, attn_sink[l], bias, seq)
        yb = _conv(proj, conv_w[l], row(conv_b[l]), row(conv_ln_g[l]), row(conv_ln_b[l]), seq)
        yc = _gla(proj, w2[l], row(gla_gate_b[l]), row(gla_norm_g[l]), seq)
        xt = _merge(xt, ya, yb, yc, proj, wa, wb, wc, wo, row(mix_post_g[l]), l)
        xt = _ffn(xt, row(ffn2_pre_g[l]), row(ffn2_post_g[l]), f2g, f2u, f2d, l)
    return xt.reshape(batch, seq, d)
```

```python
import functools
import math

import jax
import jax.numpy as jnp
from jax import lax
from jax.experimental import pallas as pl
from jax.experimental.pallas import tpu as pltpu

EPS = 1e-6
D_MODEL = 2048
D_FF = 5632
HEAD_DIM = 64
N_HEADS = 16
N_KV_HEADS = 4
ATTN_BLOCK = 128
REL_BUCKETS = 32
REL_MAX_DIST = 128
CONV_CH = 1024
CONV_WIDTH = 31
GLA_HEADS = 4
GLA_DK = 128
GLA_DV = 256
GLA_CHUNK = 64
GLA_RANK = 16
GLA_GATE_NORM = 16.0
MASK_VALUE = -1e30

LANES = 128
SUBLANES = 8
VMEM_LIMIT_BYTES = 56 * 1024 * 1024

COL_GATES = 0
COL_MAIN = 3 * D_MODEL
COL_Q = COL_MAIN
COL_K = COL_MAIN + 1024
COL_V = COL_MAIN + 1280
COL_CONV_A = COL_MAIN + 1536
COL_CONV_G = COL_MAIN + 2560
COL_GLA_Q = COL_MAIN + 3584
COL_GLA_K = COL_MAIN + 4096
COL_GLA_V = COL_MAIN + 4608
COL_GLA_G = COL_MAIN + 5632
COL_LR = COL_MAIN + 6656
MAIN_COLS = 6672
MAIN_WIDTH = 7168
HALF = 512
PROJ_BLOCK_N = 1024

FFN_BLOCK_M = 1024
FFN_BLOCK_F = 512
FFN_SPLIT = 2
FFN_SLICES = 8
FFN_SLICE_ALIGN = 16
PROJ_BLOCK_M = 1024
ATTN_TOKENS = 1024
CONV_TOKENS = 512
CONV_HALO = 32
CONV_ROWS = 32
CONV_COLS = 512
CONV_UNROLL = 4
GLA_TOKENS = 512
GLA_UNROLL = 8
MERGE_BLOCK_M = 256


def _params(*semantics):
    return pltpu.CompilerParams(dimension_semantics=semantics, vmem_limit_bytes=VMEM_LIMIT_BYTES)


def _rms(x, g):
    return x * lax.rsqrt(jnp.mean(x * x, axis=-1, keepdims=True) + EPS) * g


def _dot(a, b):
    return jnp.dot(a, b, preferred_element_type=jnp.float32)


def _dot_nt(a, b):
    return lax.dot_general(a, b, (((1,), (1,)), ((), ())), preferred_element_type=jnp.float32)


def _dot_tn(a, b):
    return lax.dot_general(a, b, (((0,), (0,)), ((), ())), preferred_element_type=jnp.float32)


def _ffn_kernel(xp_ref, xn_ref, gpre_ref, gpost_ref, wg_ref, wu_ref, wd_ref, o_ref, h_scr, acc_scr, *, nb):
    r, f = pl.program_id(0), pl.program_id(1)
    rows = pl.ds(pl.multiple_of(jnp.minimum(f, FFN_SLICES - 1) * xn_ref.shape[0], xn_ref.shape[0]), xn_ref.shape[0])

    @pl.when((r == 0) & (f == 0))
    def _():
        acc_scr[...] = jnp.zeros_like(acc_scr)

    def finish(slot):
        o_ref[...] = xp_ref[...] + 0.5 * _rms(acc_scr[slot, rows, :], gpost_ref[...])

    def prepare(slot):
        h_scr[slot, rows, :] = _rms(xn_ref[...], gpre_ref[...]).astype(jnp.bfloat16)

    def up(slot):
        h = h_scr[slot]
        acts = []
        for s in range(FFN_SPLIT):
            cols = slice(s * wg_ref.shape[1] // FFN_SPLIT, (s + 1) * wg_ref.shape[1] // FFN_SPLIT)
            a = _dot(h, wg_ref[:, cols])
            u = _dot(h, wu_ref[:, cols])
            acts.append((a * jax.nn.sigmoid(a) * u).astype(jnp.bfloat16))
        return jnp.concatenate(acts, axis=1)

    def down(slot, act):
        carried = jnp.where(f > 0, acc_scr[slot], 0.0)
        acc_scr[slot] = carried + _dot(act, wd_ref[...])

    @pl.when(r == 0)
    def _():
        prepare(0)

    for parity in (0, 1):
        @pl.when((r >= 1) & (r <= nb) & ((r - 1) % 2 == parity))
        def _():
            finish(1 - parity)
            act = up(parity)
            prepare(1 - parity)
            down(parity, act)

    @pl.when(r == nb + 1)
    def _():
        finish((nb - 1) % 2)


def _ffn(x, gpre, gpost, wg, wu, wd, l):
    nt, d = x.shape
    ff = wg.shape[2]
    bm = min(FFN_BLOCK_M, nt)
    bf = FFN_BLOCK_F
    nb, nf = nt // bm, ff // bf
    rs = bm // FFN_SLICES
    assert nf >= FFN_SLICES and rs % FFN_SLICE_ALIGN == 0

    def row_slice(block, f):
        return jnp.clip(block, 0, nb - 1) * FFN_SLICES + jnp.minimum(f, FFN_SLICES - 1)

    def chunk(r, f):
        return jnp.where(r == 0, 0, jnp.where(r == nb + 1, nf - 1, f))

    return pl.pallas_call(
        functools.partial(_ffn_kernel, nb=nb),
        grid=(nb + 2, nf),
        in_specs=[
            pl.BlockSpec((rs, d), lambda r, f: (row_slice(r - 2, f), 0)),
            pl.BlockSpec((rs, d), lambda r, f: (row_slice(r, f), 0)),
            pl.BlockSpec((1, d), lambda r, f: (0, 0)),
            pl.BlockSpec((1, d), lambda r, f: (0, 0)),
            pl.BlockSpec((None, d, bf), lambda r, f: (l, 0, chunk(r, f))),
            pl.BlockSpec((None, d, bf), lambda r, f: (l, 0, chunk(r, f))),
            pl.BlockSpec((None, bf, d), lambda r, f: (l, chunk(r, f), 0)),
        ],
        out_specs=pl.BlockSpec((rs, d), lambda r, f: (jnp.where(r < 2, 0, row_slice(r - 2, f)), 0)),
        out_shape=jax.ShapeDtypeStruct((nt, d), jnp.float32),
        scratch_shapes=[pltpu.VMEM((2, bm, d), jnp.bfloat16), pltpu.VMEM((2, bm, d), jnp.float32)],
        compiler_params=_params("arbitrary", "arbitrary"),
        name="ffn",
    )(x, x, gpre, gpost, wg, wu, wd)


def _proj_kernel(x_ref, g_ref, wg_ref, wm_ref, o_ref, h_scr, *, gate_blocks):
    j = pl.program_id(1)

    @pl.when(j == 0)
    def _():
        h_scr[...] = _rms(x_ref[...], g_ref[...]).astype(jnp.bfloat16)

    @pl.when(j < gate_blocks)
    def _():
        o_ref[...] = _dot(h_scr[...], wg_ref[...]).astype(o_ref.dtype)

    @pl.when(j >= gate_blocks)
    def _():
        o_ref[...] = _dot(h_scr[...], wm_ref[...]).astype(o_ref.dtype)


def _proj(x, g, w_gates, w_main, l):
    nt, d = x.shape
    bm = min(PROJ_BLOCK_M, nt)
    bn = PROJ_BLOCK_N
    gate_blocks, main_blocks = w_gates.shape[2] // bn, w_main.shape[2] // bn
    return pl.pallas_call(
        functools.partial(_proj_kernel, gate_blocks=gate_blocks),
        grid=(nt // bm, gate_blocks + main_blocks),
        in_specs=[
            pl.BlockSpec((bm, d), lambda i, j: (i, 0)),
            pl.BlockSpec((1, d), lambda i, j: (0, 0)),
            pl.BlockSpec((None, d, bn), lambda i, j: (l, 0, jnp.minimum(j, gate_blocks - 1))),
            pl.BlockSpec((None, d, bn), lambda i, j: (
                l, 0, jnp.where(j < gate_blocks - 1, main_blocks - 1, jnp.maximum(j - gate_blocks, 0)))),
        ],
        out_specs=pl.BlockSpec((bm, bn), lambda i, j: (i, j)),
        out_shape=jax.ShapeDtypeStruct((nt, (gate_blocks + main_blocks) * bn), jnp.bfloat16),
        scratch_shapes=[pltpu.VMEM((bm, d), jnp.bfloat16)],
        compiler_params=_params("parallel", "arbitrary"),
        name="proj",
    )(x, g, w_gates, w_main)


def _rel_buckets():
    qi = jnp.arange(ATTN_BLOCK)[None, :]
    kj = jnp.arange(2 * ATTN_BLOCK)[:, None]
    n = jnp.maximum(ATTN_BLOCK + qi - kj, 0)
    max_exact = REL_BUCKETS // 2
    nf = jnp.maximum(n, 1).astype(jnp.float32)
    large = max_exact + (jnp.log(nf / max_exact) / math.log(REL_MAX_DIST / max_exact)
                         * (REL_BUCKETS - max_exact)).astype(jnp.int32)
    large = jnp.minimum(large, REL_BUCKETS - 1)
    return jnp.where(n < max_exact, n, large).astype(jnp.int32)


def _bias_kernel(rel_ref, bucket_ref, o_ref):
    h = pl.program_id(0)
    bucket = bucket_ref[...]
    kj = lax.broadcasted_iota(jnp.int32, bucket.shape, 0)
    qi = lax.broadcasted_iota(jnp.int32, bucket.shape, 1)
    dist = ATTN_BLOCK + qi - kj
    acc = jnp.zeros(bucket.shape, jnp.float32)
    for b in range(REL_BUCKETS):
        acc = jnp.where(bucket == b, rel_ref[b, h], acc)
    o_ref[0] = jnp.where((dist >= 0) & (dist < ATTN_BLOCK), acc, MASK_VALUE)


def _band_bias(rel_bias):
    shape = (2 * ATTN_BLOCK, ATTN_BLOCK)
    return pl.pallas_call(
        _bias_kernel,
        grid=(N_HEADS,),
        in_specs=[
            pl.BlockSpec(memory_space=pltpu.SMEM),
            pl.BlockSpec(shape, lambda h: (0, 0)),
        ],
        out_specs=pl.BlockSpec((1,) + shape, lambda h: (h, 0, 0)),
        out_shape=jax.ShapeDtypeStruct((N_HEADS,) + shape, jnp.float32),
        compiler_params=_params("arbitrary"),
        name="band_bias",
    )(rel_bias, _rel_buckets())


def _lane_swap_halves(x):
    y = pltpu.roll(pltpu.bitcast(x, jnp.uint32), HEAD_DIM, axis=1)
    return pltpu.bitcast(y, jnp.bfloat16)


def _head_pair_tiles(t128, head_in_high_lanes):
    lane = lax.broadcasted_iota(jnp.int32, t128.shape, 1)
    zero = jnp.zeros_like(t128)
    if head_in_high_lanes:
        hi = jnp.where(lane >= HEAD_DIM, t128, zero)
        return _lane_swap_halves(hi), hi
    lo = jnp.where(lane < HEAD_DIM, t128, zero)
    return lo, _lane_swap_halves(lo)


def _attn_kernel(sink_ref, q_ref, k_ref, v_ref, kp_ref, vp_ref, bias_ref, o_ref, kfull, vfull, s_even, s_odd, p_even, p_odd,
                 *, steps_per_seq):
    blk = ATTN_BLOCK
    nblk = q_ref.shape[0] // blk
    first_step = pl.program_id(0) % steps_per_seq == 0
    kfull[0:blk, :] = kp_ref[...]
    vfull[0:blk, :] = vp_ref[...]
    kfull[blk:, :] = k_ref[...]
    vfull[blk:, :] = v_ref[...]

    def first_row(n):
        return n * blk if isinstance(n, int) else pl.multiple_of(n * blk, blk)

    def scores(n, s_scr):
        r0 = first_row(n)
        kband = kfull[pl.ds(r0, 2 * blk), :]
        for j in range(N_KV_HEADS):
            c = (j // 2) * LANES
            ka, kb = _head_pair_tiles(kband[:, c:c + LANES], j % 2 == 1)
            q0 = q_ref[pl.ds(r0, blk), j * 256:j * 256 + LANES]
            q1 = q_ref[pl.ds(r0, blk), j * 256 + LANES:(j + 1) * 256]
            qs = jnp.concatenate([q0, q1], axis=0) * (HEAD_DIM ** -0.5)
            s_scr[j] = _dot_nt(jnp.concatenate([ka, kb], axis=0), qs)

    def softmax(s_scr, p_scr, maybe_no_prev=False):
        for h in range(N_HEADS):
            j, g = h // 4, h % 4
            logits = s_scr[j, (g % 2) * 2 * blk:(g % 2 + 1) * 2 * blk, (g // 2) * blk:(g // 2 + 1) * blk]
            logits = logits + bias_ref[h]
            if maybe_no_prev:
                prev_rows = jnp.where(first_step, MASK_VALUE, logits[:blk])
                logits = jnp.concatenate([prev_rows, logits[blk:]], axis=0)
            sink = sink_ref[h]
            m = jnp.maximum(jnp.max(logits, axis=0, keepdims=True), sink)
            p = jnp.exp(logits - m)
            key0 = lax.broadcasted_iota(jnp.int32, (SUBLANES, blk), 0) == 0
            top = jnp.where(key0, jnp.exp(sink - m), p[:SUBLANES])
            p_scr[h] = jnp.concatenate([top, p[SUBLANES:]], axis=0).astype(jnp.bfloat16)

    lane = lax.broadcasted_iota(jnp.int32, (2 * blk, LANES), 1)
    ones_even = (lane < HEAD_DIM).astype(jnp.bfloat16)
    ones_odd = (lane >= HEAD_DIM).astype(jnp.bfloat16)
    not_key0 = lax.broadcasted_iota(jnp.int32, (2 * blk, LANES), 0) > 0

    def values(n, p_scr):
        r0 = first_row(n)
        vband = vfull[pl.ds(r0, 2 * blk), :]
        for j in range(N_KV_HEADS):
            c = (j // 2) * LANES
            v_tile = jnp.where(not_key0, vband[:, c:c + LANES], jnp.zeros((2 * blk, LANES), jnp.bfloat16))
            va, vb = _head_pair_tiles(v_tile, j % 2 == 1)
            rhs_even = jnp.concatenate([va, ones_even], axis=1)
            rhs_odd = jnp.concatenate([vb, ones_odd], axis=1)
            for pair in range(2):
                h = j * 4 + 2 * pair
                out = _dot_tn(p_scr[h], rhs_even) + _dot_tn(p_scr[h + 1], rhs_odd)
                out = out[:, :LANES] * (1.0 / out[:, LANES:])
                col = j * 256 + pair * LANES
                o_ref[pl.ds(r0, blk), col:col + LANES] = out.astype(o_ref.dtype)

    def even_block(t, has_prev=True, maybe_no_prev=False):
        scores(t + 1, s_odd)
        softmax(s_even, p_even, maybe_no_prev)
        if has_prev:
            values(t - 1, p_odd)

    def odd_block(t, has_next=True):
        if has_next:
            scores(t + 1, s_even)
        softmax(s_odd, p_odd)
        values(t - 1, p_even)

    def block_pair(u, carry):
        t = 2 * u
        even_block(t)
        odd_block(t + 1)
        return carry

    scores(0, s_even)
    even_block(0, has_prev=False, maybe_no_prev=True)
    odd_block(1)
    lax.fori_loop(1, nblk // 2 - 1, block_pair, 0, unroll=True)
    even_block(nblk - 2)
    odd_block(nblk - 1, has_next=False)
    values(nblk - 1, p_odd)


def _attention(proj, sink, bias, seq):
    nt = proj.shape[0]
    tq = min(ATTN_TOKENS, seq)
    per = tq // ATTN_BLOCK
    assert per % 2 == 0 and per >= 4, "the block pipeline needs an even number (>= 4) of blocks per step"
    steps_per_seq = seq // tq
    cq, ck, cv = COL_Q // 1024, COL_K // 256, COL_V // 256

    def prev(i):
        return jnp.maximum(i * per - 1, 0)

    return pl.pallas_call(
        functools.partial(_attn_kernel, steps_per_seq=steps_per_seq),
        grid=(nt // tq,),
        in_specs=[
            pl.BlockSpec(memory_space=pltpu.SMEM),
            pl.BlockSpec((tq, 1024), lambda i: (i, cq)),
            pl.BlockSpec((tq, 256), lambda i: (i, ck)),
            pl.BlockSpec((tq, 256), lambda i: (i, cv)),
            pl.BlockSpec((ATTN_BLOCK, 256), lambda i: (prev(i), ck)),
            pl.BlockSpec((ATTN_BLOCK, 256), lambda i: (prev(i), cv)),
            pl.BlockSpec((N_HEADS, 2 * ATTN_BLOCK, ATTN_BLOCK), lambda i: (0, 0, 0)),
        ],
        out_specs=pl.BlockSpec((tq, 1024), lambda i: (i, 0)),
        out_shape=jax.ShapeDtypeStruct((nt, 1024), jnp.bfloat16),
        scratch_shapes=[pltpu.VMEM((tq + ATTN_BLOCK, 256), jnp.bfloat16),
                        pltpu.VMEM((tq + ATTN_BLOCK, 256), jnp.bfloat16),
                        pltpu.VMEM((N_KV_HEADS, 4 * ATTN_BLOCK, 2 * ATTN_BLOCK), jnp.float32),
                        pltpu.VMEM((N_KV_HEADS, 4 * ATTN_BLOCK, 2 * ATTN_BLOCK), jnp.float32),
                        pltpu.VMEM((N_HEADS, 2 * ATTN_BLOCK, ATTN_BLOCK), jnp.bfloat16),
                        pltpu.VMEM((N_HEADS, 2 * ATTN_BLOCK, ATTN_BLOCK), jnp.bfloat16)],
        compiler_params=_params("arbitrary"),
        name="attention",
    )(sink, proj, proj, proj, proj, proj, bias)


def _conv_kernel(a0_ref, a1_ref, g0_ref, g1_ref, ap0_ref, ap1_ref, gp0_ref, gp1_ref, w_ref, b_ref, lng_ref, lnb_ref,
                 o_ref, y_scr, c_scr, wb_scr, *, steps_per_seq):
    tm = a0_ref.shape[0]
    halo = CONV_HALO
    first_step = pl.program_id(0) % steps_per_seq == 0
    halves = ((a0_ref, g0_ref, ap0_ref, gp0_ref), (a1_ref, g1_ref, ap1_ref, gp1_ref))
    for half, (a_ref, g_ref, ap_ref, gp_ref) in enumerate(halves):
        hs = slice(half * HALF, (half + 1) * HALF)
        yp = ap_ref[...].astype(jnp.float32) * jax.nn.sigmoid(gp_ref[...].astype(jnp.float32))
        y_scr[0, 0:halo, hs] = jnp.where(first_step, 0.0, yp)
        y_scr[0, halo:, hs] = a_ref[...].astype(jnp.float32) * jax.nn.sigmoid(g_ref[...].astype(jnp.float32))
    shifted_rows = tm + halo - SUBLANES
    for r in range(1, SUBLANES):
        y_scr[r, 0:shifted_rows, :] = y_scr[0, r:r + shifted_rows, :]
    base = halo - (CONV_WIDTH - 1)

    @pl.when(pl.program_id(0) == 0)
    def _():
        for j in range(CONV_WIDTH):
            wb_scr[j] = jnp.broadcast_to(w_ref[j:j + 1, :], (SUBLANES, CONV_CH))
        wb_scr[CONV_WIDTH] = jnp.broadcast_to(b_ref[...], (SUBLANES, CONV_CH))

    lane_chunks = CONV_CH // CONV_COLS
    row_groups = CONV_ROWS // SUBLANES

    def tile(t, carry):
        r0 = pl.multiple_of((t // lane_chunks) * CONV_ROWS, CONV_ROWS)
        cs = pl.ds(pl.multiple_of((t % lane_chunks) * CONV_COLS, CONV_COLS), CONV_COLS)
        accs = [wb_scr[CONV_WIDTH, :, cs]] * row_groups
        for phase in range(SUBLANES):
            taps = [j for j in range(CONV_WIDTH) if (base + j) % SUBLANES == phase]
            first = [(base + j) // SUBLANES for j in taps]
            groups = {k: y_scr[phase, pl.ds(r0 + k * SUBLANES, SUBLANES), cs]
                      for k in range(min(first), max(first) + row_groups)}
            for j, k0 in zip(taps, first):
                wj = wb_scr[j, :, cs]
                for g in range(row_groups):
                    accs[g] = accs[g] + groups[k0 + g] * wj
        for g in range(row_groups):
            c_scr[pl.ds(r0 + g * SUBLANES, SUBLANES), cs] = accs[g]
        return carry

    lax.fori_loop(0, (tm // CONV_ROWS) * lane_chunks, tile, 0, unroll=CONV_UNROLL)
    y = c_scr[...]
    mu = jnp.mean(y, axis=-1, keepdims=True)
    yc = y - mu
    z = yc * lax.rsqrt(jnp.mean(yc * yc, axis=-1, keepdims=True) + EPS) * lng_ref[...] + lnb_ref[...]
    o_ref[...] = (z * jax.nn.sigmoid(z)).astype(o_ref.dtype)


def _conv(proj, w, b, ln_g, ln_b, seq):
    nt = proj.shape[0]
    tm = min(CONV_TOKENS, seq)
    steps_per_seq = seq // tm
    per = tm // CONV_HALO
    ca, cg = COL_CONV_A // HALF, COL_CONV_G // HALF

    def prev(i):
        return jnp.maximum(i * per - 1, 0)

    def cur(c):
        return pl.BlockSpec((tm, HALF), lambda i: (i, c))

    def before(c):
        return pl.BlockSpec((CONV_HALO, HALF), lambda i: (prev(i), c))

    vec = pl.BlockSpec((1, CONV_CH), lambda i: (0, 0))
    return pl.pallas_call(
        functools.partial(_conv_kernel, steps_per_seq=steps_per_seq),
        grid=(nt // tm,),
        in_specs=[
            cur(ca), cur(ca + 1), cur(cg), cur(cg + 1),
            before(ca), before(ca + 1), before(cg), before(cg + 1),
            pl.BlockSpec((CONV_WIDTH, CONV_CH), lambda i: (0, 0)),
            vec, vec, vec,
        ],
        out_specs=pl.BlockSpec((tm, CONV_CH), lambda i: (i, 0)),
        out_shape=jax.ShapeDtypeStruct((nt, CONV_CH), jnp.bfloat16),
        scratch_shapes=[pltpu.VMEM((SUBLANES, tm + CONV_HALO, CONV_CH), jnp.float32),
                        pltpu.VMEM((tm, CONV_CH), jnp.float32),
                        pltpu.VMEM((CONV_WIDTH + 1, SUBLANES, CONV_CH), jnp.float32)],
        compiler_params=_params("arbitrary"),
        name="conformer_conv",
    )(*([proj] * 8), w, b, ln_g, ln_b)


def _split_bf16(x):
    hi = x.astype(jnp.bfloat16)
    lo = (x - hi.astype(jnp.float32)).astype(jnp.bfloat16)
    return hi, lo


def _gla_kernel(q_ref, k_ref, v0_ref, v1_ref, g0_ref, g1_ref, lr_ref, w2_ref, gb_ref, ng_ref, o_ref,
                state_t, tri_scr, qe_scr, ke_scr, b_scr, oi_scr, kvt_scr, *, steps_per_seq):
    L = GLA_CHUNK
    tm = q_ref.shape[0]
    nchunk = tm // L

    @pl.when(pl.program_id(0) == 0)
    def _():
        row = lax.broadcasted_iota(jnp.int32, (tm, tm), 0)
        col = lax.broadcasted_iota(jnp.int32, (tm, tm), 1)
        tri_scr[...] = ((row // L == col // L) & (row >= col)).astype(jnp.bfloat16)

    @pl.when(pl.program_id(0) % steps_per_seq == 0)
    def _():
        state_t[...] = jnp.zeros_like(state_t)

    z = _dot(lr_ref[...], w2_ref[...]) + gb_ref[...]
    gk = (jnp.minimum(z, 0.0) - jnp.log(1.0 + jnp.exp(-jnp.abs(z)))) * (1.0 / GLA_GATE_NORM)
    gk_hi, gk_lo = _split_bf16(gk)
    b = _dot(tri_scr[...], gk_hi) + _dot(tri_scr[...], gk_lo)
    b_scr[...] = b
    qe_scr[...] = (q_ref[...].astype(jnp.float32) * (GLA_DK ** -0.5) * jnp.exp(b)).astype(jnp.bfloat16)
    ke_scr[...] = (k_ref[...].astype(jnp.float32) * jnp.exp(-b)).astype(jnp.bfloat16)

    causal = lax.broadcasted_iota(jnp.int32, (L, L), 0) >= lax.broadcasted_iota(jnp.int32, (L, L), 1)

    def head_cols(halves, rs, h):
        per_half = HALF // GLA_DV
        return halves[h // per_half][rs, (h % per_half) * GLA_DV:(h % per_half + 1) * GLA_DV]

    def local(c, carry):
        rs = pl.ds(pl.multiple_of(c * L, L), L)
        heads = [(slice(h * GLA_DK, (h + 1) * GLA_DK), slice(h * GLA_DV, (h + 1) * GLA_DV)) for h in range(GLA_HEADS)]
        scores = [_dot_nt(qe_scr[rs, ks], ke_scr[rs, ks]) for ks, _ in heads]
        for h, (ks, vs) in enumerate(heads):
            kvt_scr[c, h] = _dot_tn(head_cols((v0_ref, v1_ref), rs, h), ke_scr[rs, ks])
        atts = [jnp.where(causal, s, 0.0).astype(jnp.bfloat16) for s in scores]
        for h, (_, vs) in enumerate(heads):
            oi_scr[rs, vs] = _dot(atts[h], head_cols((v0_ref, v1_ref), rs, h))
        return carry

    lax.fori_loop(0, nchunk, local, 0, unroll=GLA_UNROLL)

    def recur(c, carry):
        r0 = pl.multiple_of(c * L, L)
        rs = pl.ds(r0, L)
        b_tail = b_scr[pl.ds(r0 + L - SUBLANES, SUBLANES), :]
        for h in range(GLA_HEADS):
            ks = slice(h * GLA_DK, (h + 1) * GLA_DK)
            vs = slice(h * GLA_DV, (h + 1) * GLA_DV)
            s_old = state_t[h]
            o = oi_scr[rs, vs] + _dot_nt(qe_scr[rs, ks], s_old.astype(jnp.bfloat16))
            decay = jnp.exp(b_tail[SUBLANES - 1:SUBLANES, ks])
            state_t[h] = decay * (s_old + kvt_scr[c, h])
            o = _rms(o, ng_ref[...])
            gate = head_cols((g0_ref, g1_ref), rs, h).astype(jnp.float32)
            o_ref[rs, vs] = (o * (gate * jax.nn.sigmoid(gate))).astype(o_ref.dtype)
        return carry

    lax.fori_loop(0, nchunk, recur, 0, unroll=GLA_UNROLL)


def _gla(proj, w2, gate_b, norm_g, seq):
    nt = proj.shape[0]
    tm = min(GLA_TOKENS, seq)
    steps_per_seq = seq // tm
    return pl.pallas_call(
        functools.partial(_gla_kernel, steps_per_seq=steps_per_seq),
        grid=(nt // tm,),
        in_specs=[
            pl.BlockSpec((tm, 512), lambda i: (i, COL_GLA_Q // 512)),
            pl.BlockSpec((tm, 512), lambda i: (i, COL_GLA_K // 512)),
            pl.BlockSpec((tm, HALF), lambda i: (i, COL_GLA_V // HALF)),
            pl.BlockSpec((tm, HALF), lambda i: (i, COL_GLA_V // HALF + 1)),
            pl.BlockSpec((tm, HALF), lambda i: (i, COL_GLA_G // HALF)),
            pl.BlockSpec((tm, HALF), lambda i: (i, COL_GLA_G // HALF + 1)),
            pl.BlockSpec((tm, LANES), lambda i: (i, COL_LR // LANES)),
            pl.BlockSpec((LANES, 512), lambda i: (0, 0)),
            pl.BlockSpec((1, 512), lambda i: (0, 0)),
            pl.BlockSpec((1, GLA_DV), lambda i: (0, 0)),
        ],
        out_specs=pl.BlockSpec((tm, 1024), lambda i: (i, 0)),
        out_shape=jax.ShapeDtypeStruct((nt, 1024), jnp.bfloat16),
        scratch_shapes=[
            pltpu.VMEM((GLA_HEADS, GLA_DV, GLA_DK), jnp.float32),
            pltpu.VMEM((tm, tm), jnp.bfloat16),
            pltpu.VMEM((tm, GLA_HEADS * GLA_DK), jnp.bfloat16),
            pltpu.VMEM((tm, GLA_HEADS * GLA_DK), jnp.bfloat16),
            pltpu.VMEM((tm, GLA_HEADS * GLA_DK), jnp.float32),
            pltpu.VMEM((tm, GLA_HEADS * GLA_DV), jnp.float32),
            pltpu.VMEM((tm // GLA_CHUNK, GLA_HEADS, GLA_DV, GLA_DK), jnp.float32),
        ],
        compiler_params=_params("arbitrary"),
        name="gla",
    )(*([proj] * 7), w2, gate_b, norm_g)


def _merge_kernel(x_ref, ya_ref, yb_ref, yc_ref, ga_ref, gb_ref, gc_ref, wa_ref, wb_ref, wc_ref, wo_ref, g_ref, o_ref,
                  m_scr, *, nb):
    t = pl.program_id(0)

    @pl.when(t == 0)
    def _():
        m_scr[1] = jnp.zeros(m_scr.shape[1:], m_scr.dtype)

    def branch(gate_ref, y_ref, w_ref):
        return jax.nn.sigmoid(gate_ref[...].astype(jnp.float32)) * _dot(y_ref[...], w_ref[...])

    def matmuls(slot):
        merged = branch(ga_ref, ya_ref, wa_ref) + branch(gb_ref, yb_ref, wb_ref) + branch(gc_ref, yc_ref, wc_ref)
        m_scr[slot] = _dot(merged.astype(jnp.bfloat16), wo_ref[...])

    def finish(slot):
        o_ref[...] = x_ref[...] + _rms(m_scr[slot], g_ref[...])

    for parity in (0, 1):
        @pl.when((t < nb) & (t % 2 == parity))
        def _():
            finish(1 - parity)
            matmuls(parity)

    @pl.when(t == nb)
    def _():
        finish((nb - 1) % 2)


def _merge(x, ya, yb, yc, proj, wa, wb, wc, wo, g, l):
    nt, d = x.shape
    bm = min(MERGE_BLOCK_M, nt)
    nb = nt // bm
    c0 = COL_GATES // d

    def resident(shape):
        return pl.BlockSpec((None,) + shape, lambda t: (l, 0, 0), pipeline_mode=pl.Buffered(1))

    def cur(t):
        return jnp.minimum(t, nb - 1)

    def before(t):
        return jnp.maximum(t - 1, 0)

    def branch_in():
        return pl.BlockSpec((bm, 1024), lambda t: (cur(t), 0))

    return pl.pallas_call(
        functools.partial(_merge_kernel, nb=nb),
        grid=(nb + 1,),
        in_specs=[
            pl.BlockSpec((bm, d), lambda t: (before(t), 0)),
            branch_in(), branch_in(), branch_in(),
            pl.BlockSpec((bm, d), lambda t: (cur(t), c0)),
            pl.BlockSpec((bm, d), lambda t: (cur(t), c0 + 1)),
            pl.BlockSpec((bm, d), lambda t: (cur(t), c0 + 2)),
            resident((1024, d)), resident((1024, d)), resident((1024, d)), resident((d, d)),
            pl.BlockSpec((1, d), lambda t: (0, 0)),
        ],
        out_specs=pl.BlockSpec((bm, d), lambda t: (before(t), 0)),
        out_shape=jax.ShapeDtypeStruct((nt, d), jnp.float32),
        scratch_shapes=[pltpu.VMEM((2, bm, d), jnp.float32)],
        compiler_params=_params("arbitrary"),
        name="merge",
    )(x, ya, yb, yc, proj, proj, proj, wa, wb, wc, wo, g)


def _split_w_in(w_in):
    gates = w_in[..., MAIN_COLS:].astype(jnp.bfloat16)
    main = jnp.pad(w_in[..., :MAIN_COLS].astype(jnp.bfloat16), ((0, 0), (0, 0), (0, MAIN_WIDTH - MAIN_COLS)))
    return gates, main


def kernel(x, rel_bias, ffn1_pre_g, ffn1_post_g, ffn1_w_gate, ffn1_w_up, ffn1_w_down, mix_pre_g, mix_post_g, w_in, attn_sink, conv_w, conv_b, conv_ln_g, conv_ln_b, gla_gate_w2, gla_gate_b, gla_norm_g, w_a_up, w_b_up, w_c_up, w_out, ffn2_pre_g, ffn2_post_g, ffn2_w_gate, ffn2_w_up, ffn2_w_down):
    batch, seq, d = x.shape
    depth = w_in.shape[0]
    bf = lambda w: w.astype(jnp.bfloat16)
    row = lambda v: v.reshape(1, -1)
    bias = _band_bias(rel_bias)
    f1g, f1u, f1d = bf(ffn1_w_gate), bf(ffn1_w_up), bf(ffn1_w_down)
    f2g, f2u, f2d = bf(ffn2_w_gate), bf(ffn2_w_up), bf(ffn2_w_down)
    w_gates, w_main = _split_w_in(w_in)
    wa, wb, wc, wo = bf(w_a_up), bf(w_b_up), bf(w_c_up), bf(w_out)
    w2 = jnp.pad(bf(gla_gate_w2), ((0, 0), (0, LANES - GLA_RANK), (0, 0)))
    xt = x.reshape(batch * seq, d)
    for l in range(depth):
        xt = _ffn(xt, row(ffn1_pre_g[l]), row(ffn1_post_g[l]), f1g, f1u, f1d, l)
        proj = _proj(xt, row(mix_pre_g[l]), w_gates, w_main, l)
        ya = _attention(proj, attn_sink[l], bias, seq)
        yb = _conv(proj, conv_w[l], row(conv_b[l]), row(conv_ln_g[l]), row(conv_ln_b[l]), seq)
        yc = _gla(proj, w2[l], row(gla_gate_b[l]), row(gla_norm_g[l]), seq)
        xt = _merge(xt, ya, yb, yc, proj, wa, wb, wc, wo, row(mix_post_g[l]), l)
        xt = _ffn(xt, row(ffn2_pre_g[l]), row(ffn2_post_g[l]), f2g, f2u, f2d, l)
    return xt.reshape(batch, seq, d)
```

```python
import functools
import math

import jax
import jax.numpy as jnp
from jax import lax
from jax.experimental import pallas as pl
from jax.experimental.pallas import tpu as pltpu

EPS = 1e-6
D_MODEL = 2048
D_FF = 5632
HEAD_DIM = 64
N_HEADS = 16
N_KV_HEADS = 4
ATTN_BLOCK = 128
REL_BUCKETS = 32
REL_MAX_DIST = 128
CONV_CH = 1024
CONV_WIDTH = 31
GLA_HEADS = 4
GLA_DK = 128
GLA_DV = 256
GLA_CHUNK = 64
GLA_RANK = 16
GLA_GATE_NORM = 16.0
MASK_VALUE = -1e30

LANES = 128
SUBLANES = 8
VMEM_LIMIT_BYTES = 56 * 1024 * 1024

COL_GATES = 0
COL_MAIN = 3 * D_MODEL
COL_Q = COL_MAIN
COL_K = COL_MAIN + 1024
COL_V = COL_MAIN + 1280
COL_CONV_A = COL_MAIN + 1536
COL_CONV_G = COL_MAIN + 2560
COL_GLA_Q = COL_MAIN + 3584
COL_GLA_K = COL_MAIN + 4096
COL_GLA_V = COL_MAIN + 4608
COL_GLA_G = COL_MAIN + 5632
COL_LR = COL_MAIN + 6656
MAIN_COLS = 6672
MAIN_WIDTH = 7168
HALF = 512
PROJ_BLOCK_N = 1024

FFN_BLOCK_M = 1024
FFN_BLOCK_F = 512
FFN_SPLIT = 2
FFN_SLICES = 8
FFN_SLICE_ALIGN = 16
PROJ_BLOCK_M = 1024
ATTN_TOKENS = 1024
CONV_TOKENS = 512
CONV_HALO = 32
CONV_ROWS = 32
CONV_COLS = 512
CONV_UNROLL = 4
GLA_TOKENS = 512
GLA_UNROLL = 8
MERGE_BLOCK_M = 256


def _params(*semantics):
    return pltpu.CompilerParams(dimension_semantics=semantics, vmem_limit_bytes=VMEM_LIMIT_BYTES)


def _rms(x, g):
    return x * lax.rsqrt(jnp.mean(x * x, axis=-1, keepdims=True) + EPS) * g


def _dot(a, b):
    return jnp.dot(a, b, preferred_element_type=jnp.float32)


def _dot_nt(a, b):
    return lax.dot_general(a, b, (((1,), (1,)), ((), ())), preferred_element_type=jnp.float32)


def _dot_tn(a, b):
    return lax.dot_general(a, b, (((0,), (0,)), ((), ())), preferred_element_type=jnp.float32)


def _ffn_kernel(*refs, nb, emit_next):
    if emit_next:
        (xp_ref, xn_ref, gpre_ref, gpost_ref, gnext_ref, wg_ref, wu_ref, wd_ref, o_ref, hn_ref,
         h_even, h_odd, acc_even, acc_odd) = refs
    else:
        xp_ref, xn_ref, gpre_ref, gpost_ref, wg_ref, wu_ref, wd_ref, o_ref, h_even, h_odd, acc_even, acc_odd = refs
    _ffn_body(xp_ref, xn_ref, gpre_ref, gpost_ref, wg_ref, wu_ref, wd_ref, o_ref, h_even, h_odd, acc_even, acc_odd,
              (gnext_ref, hn_ref) if emit_next else None, nb)


def _ffn_body(xp_ref, xn_ref, gpre_ref, gpost_ref, wg_ref, wu_ref, wd_ref, o_ref, h_even, h_odd, acc_even, acc_odd,
              next_refs, nb):
    h_scr, acc_scr = (h_even, h_odd), (acc_even, acc_odd)
    r, f = pl.program_id(0), pl.program_id(1)
    rows = pl.ds(pl.multiple_of(jnp.minimum(f, FFN_SLICES - 1) * xn_ref.shape[0], xn_ref.shape[0]), xn_ref.shape[0])

    @pl.when((r == 0) & (f == 0))
    def _():
        acc_even[...] = jnp.zeros_like(acc_even)
        acc_odd[...] = jnp.zeros_like(acc_odd)

    def finish(slot):
        y = xp_ref[...] + 0.5 * _rms(acc_scr[slot][rows, :], gpost_ref[...])
        o_ref[...] = y
        if next_refs is not None:
            gnext_ref, hn_ref = next_refs
            hn_ref[...] = _rms(y, gnext_ref[...]).astype(hn_ref.dtype)

    def prepare(slot):
        h_scr[slot][rows, :] = _rms(xn_ref[...], gpre_ref[...]).astype(jnp.bfloat16)

    def up(slot):
        h = h_scr[slot][...]
        acts = []
        for s in range(FFN_SPLIT):
            cols = slice(s * wg_ref.shape[1] // FFN_SPLIT, (s + 1) * wg_ref.shape[1] // FFN_SPLIT)
            a = _dot(h, wg_ref[:, cols])
            u = _dot(h, wu_ref[:, cols])
            acts.append((a * jax.nn.sigmoid(a) * u).astype(jnp.bfloat16))
        return jnp.concatenate(acts, axis=1)

    def down(slot, act):
        carried = jnp.where(f > 0, acc_scr[slot][...], 0.0)
        acc_scr[slot][...] = carried + _dot(act, wd_ref[...])

    @pl.when(r == 0)
    def _():
        prepare(0)

    for parity in (0, 1):
        @pl.when((r >= 1) & (r <= nb) & ((r - 1) % 2 == parity))
        def _():
            finish(1 - parity)
            prepare(1 - parity)
            down(parity, up(parity))

    @pl.when(r == nb + 1)
    def _():
        finish((nb - 1) % 2)


def _ffn(x, gpre, gpost, wg, wu, wd, l, gnext=None):
    nt, d = x.shape
    ff = wg.shape[2]
    bm = min(FFN_BLOCK_M, nt)
    bf = FFN_BLOCK_F
    nb, nf = nt // bm, ff // bf
    rs = bm // FFN_SLICES
    assert nf >= FFN_SLICES and rs % FFN_SLICE_ALIGN == 0

    def row_slice(block, f):
        return jnp.clip(block, 0, nb - 1) * FFN_SLICES + jnp.minimum(f, FFN_SLICES - 1)

    def chunk(r, f):
        return jnp.where(r == 0, 0, jnp.where(r == nb + 1, nf - 1, f))

    emit_next = gnext is not None
    vec = pl.BlockSpec((1, d), lambda r, f: (0, 0))
    out_spec = pl.BlockSpec((rs, d), lambda r, f: (jnp.where(r < 2, 0, row_slice(r - 2, f)), 0))
    out = pl.pallas_call(
        functools.partial(_ffn_kernel, nb=nb, emit_next=emit_next),
        grid=(nb + 2, nf),
        in_specs=[
            pl.BlockSpec((rs, d), lambda r, f: (row_slice(r - 2, f), 0)),
            pl.BlockSpec((rs, d), lambda r, f: (row_slice(r, f), 0)),
            vec, vec, *([vec] if emit_next else []),
            pl.BlockSpec((None, d, bf), lambda r, f: (l, 0, chunk(r, f))),
            pl.BlockSpec((None, d, bf), lambda r, f: (l, 0, chunk(r, f))),
            pl.BlockSpec((None, bf, d), lambda r, f: (l, chunk(r, f), 0)),
        ],
        out_specs=[out_spec, out_spec] if emit_next else out_spec,
        out_shape=([jax.ShapeDtypeStruct((nt, d), jnp.float32), jax.ShapeDtypeStruct((nt, d), jnp.bfloat16)]
                   if emit_next else jax.ShapeDtypeStruct((nt, d), jnp.float32)),
        scratch_shapes=[pltpu.VMEM((bm, d), jnp.bfloat16), pltpu.VMEM((bm, d), jnp.bfloat16),
                        pltpu.VMEM((bm, d), jnp.float32), pltpu.VMEM((bm, d), jnp.float32)],
        compiler_params=_params("arbitrary", "arbitrary"),
        name="ffn",
    )(x, x, gpre, gpost, *([gnext] if emit_next else []), wg, wu, wd)
    return tuple(out) if emit_next else out


def _proj_kernel(h_ref, wg_ref, wm_ref, o_ref, *, gate_blocks):
    j = pl.program_id(1)

    @pl.when(j < gate_blocks)
    def _():
        o_ref[...] = _dot(h_ref[...], wg_ref[...]).astype(o_ref.dtype)

    @pl.when(j >= gate_blocks)
    def _():
        o_ref[...] = _dot(h_ref[...], wm_ref[...]).astype(o_ref.dtype)


def _proj(h, w_gates, w_main, l):
    nt, d = h.shape
    bm = min(PROJ_BLOCK_M, nt)
    bn = PROJ_BLOCK_N
    gate_blocks, main_blocks = w_gates.shape[2] // bn, w_main.shape[2] // bn
    return pl.pallas_call(
        functools.partial(_proj_kernel, gate_blocks=gate_blocks),
        grid=(nt // bm, gate_blocks + main_blocks),
        in_specs=[
            pl.BlockSpec((bm, d), lambda i, j: (i, 0)),
            pl.BlockSpec((None, d, bn), lambda i, j: (l, 0, jnp.minimum(j, gate_blocks - 1))),
            pl.BlockSpec((None, d, bn), lambda i, j: (
                l, 0, jnp.where(j < gate_blocks - 1, main_blocks - 1, jnp.maximum(j - gate_blocks, 0)))),
        ],
        out_specs=pl.BlockSpec((bm, bn), lambda i, j: (i, j)),
        out_shape=jax.ShapeDtypeStruct((nt, (gate_blocks + main_blocks) * bn), jnp.bfloat16),
        compiler_params=_params("parallel", "arbitrary"),
        name="proj",
    )(h, w_gates, w_main)


def _rel_buckets():
    qi = jnp.arange(ATTN_BLOCK)[None, :]
    kj = jnp.arange(2 * ATTN_BLOCK)[:, None]
    n = jnp.maximum(ATTN_BLOCK + qi - kj, 0)
    max_exact = REL_BUCKETS // 2
    nf = jnp.maximum(n, 1).astype(jnp.float32)
    large = max_exact + (jnp.log(nf / max_exact) / math.log(REL_MAX_DIST / max_exact)
                         * (REL_BUCKETS - max_exact)).astype(jnp.int32)
    large = jnp.minimum(large, REL_BUCKETS - 1)
    return jnp.where(n < max_exact, n, large).astype(jnp.int32)


def _bias_kernel(rel_ref, bucket_ref, o_ref):
    h = pl.program_id(0)
    bucket = bucket_ref[...]
    kj = lax.broadcasted_iota(jnp.int32, bucket.shape, 0)
    qi = lax.broadcasted_iota(jnp.int32, bucket.shape, 1)
    dist = ATTN_BLOCK + qi - kj
    acc = jnp.zeros(bucket.shape, jnp.float32)
    for b in range(REL_BUCKETS):
        acc = jnp.where(bucket == b, rel_ref[b, h], acc)
    o_ref[0] = jnp.where((dist >= 0) & (dist < ATTN_BLOCK), acc, MASK_VALUE)


def _band_bias(rel_bias):
    shape = (2 * ATTN_BLOCK, ATTN_BLOCK)
    return pl.pallas_call(
        _bias_kernel,
        grid=(N_HEADS,),
        in_specs=[
            pl.BlockSpec(memory_space=pltpu.SMEM),
            pl.BlockSpec(shape, lambda h: (0, 0)),
        ],
        out_specs=pl.BlockSpec((1,) + shape, lambda h: (h, 0, 0)),
        out_shape=jax.ShapeDtypeStruct((N_HEADS,) + shape, jnp.float32),
        compiler_params=_params("arbitrary"),
        name="band_bias",
    )(rel_bias, _rel_buckets())


def _lane_swap_halves(x):
    y = pltpu.roll(pltpu.bitcast(x, jnp.uint32), HEAD_DIM, axis=1)
    return pltpu.bitcast(y, jnp.bfloat16)


def _head_pair_tiles(t128, head_in_high_lanes):
    lane = lax.broadcasted_iota(jnp.int32, t128.shape, 1)
    zero = jnp.zeros_like(t128)
    if head_in_high_lanes:
        hi = jnp.where(lane >= HEAD_DIM, t128, zero)
        return _lane_swap_halves(hi), hi
    lo = jnp.where(lane < HEAD_DIM, t128, zero)
    return lo, _lane_swap_halves(lo)


def _attn_kernel(sink_ref, q_ref, k_ref, v_ref, kp_ref, vp_ref, bias_ref, o_ref, kfull, vfull, s_even, s_odd, p_even, p_odd,
                 *, steps_per_seq):
    blk = ATTN_BLOCK
    nblk = q_ref.shape[0] // blk
    first_step = pl.program_id(0) % steps_per_seq == 0
    kfull[0:blk, :] = kp_ref[...]
    vfull[0:blk, :] = vp_ref[...]
    kfull[blk:, :] = k_ref[...]
    vfull[blk:, :] = v_ref[...]

    def first_row(n):
        return n * blk if isinstance(n, int) else pl.multiple_of(n * blk, blk)

    def scores(n, s_scr):
        r0 = first_row(n)
        kband = kfull[pl.ds(r0, 2 * blk), :]
        for j in range(N_KV_HEADS):
            c = (j // 2) * LANES
            ka, kb = _head_pair_tiles(kband[:, c:c + LANES], j % 2 == 1)
            q0 = q_ref[pl.ds(r0, blk), j * 256:j * 256 + LANES]
            q1 = q_ref[pl.ds(r0, blk), j * 256 + LANES:(j + 1) * 256]
            qs = jnp.concatenate([q0, q1], axis=0) * (HEAD_DIM ** -0.5)
            s_scr[j] = _dot_nt(jnp.concatenate([ka, kb], axis=0), qs)

    def softmax(s_scr, p_scr, maybe_no_prev=False):
        for h in range(N_HEADS):
            j, g = h // 4, h % 4
            logits = s_scr[j, (g % 2) * 2 * blk:(g % 2 + 1) * 2 * blk, (g // 2) * blk:(g // 2 + 1) * blk]
            logits = logits + bias_ref[h]
            if maybe_no_prev:
                prev_rows = jnp.where(first_step, MASK_VALUE, logits[:blk])
                logits = jnp.concatenate([prev_rows, logits[blk:]], axis=0)
            sink = sink_ref[h]
            m = jnp.maximum(jnp.max(logits, axis=0, keepdims=True), sink)
            p = jnp.exp(logits - m)
            key0 = lax.broadcasted_iota(jnp.int32, (SUBLANES, blk), 0) == 0
            top = jnp.where(key0, jnp.exp(sink - m), p[:SUBLANES])
            p_scr[h] = jnp.concatenate([top, p[SUBLANES:]], axis=0).astype(jnp.bfloat16)

    lane = lax.broadcasted_iota(jnp.int32, (2 * blk, LANES), 1)
    ones_even = (lane < HEAD_DIM).astype(jnp.bfloat16)
    ones_odd = (lane >= HEAD_DIM).astype(jnp.bfloat16)
    not_key0 = lax.broadcasted_iota(jnp.int32, (2 * blk, LANES), 0) > 0

    def values(n, p_scr):
        r0 = first_row(n)
        vband = vfull[pl.ds(r0, 2 * blk), :]
        for j in range(N_KV_HEADS):
            c = (j // 2) * LANES
            v_tile = jnp.where(not_key0, vband[:, c:c + LANES], jnp.zeros((2 * blk, LANES), jnp.bfloat16))
            va, vb = _head_pair_tiles(v_tile, j % 2 == 1)
            rhs_even = jnp.concatenate([va, ones_even], axis=1)
            rhs_odd = jnp.concatenate([vb, ones_odd], axis=1)
            for pair in range(2):
                h = j * 4 + 2 * pair
                out = _dot_tn(p_scr[h], rhs_even) + _dot_tn(p_scr[h + 1], rhs_odd)
                out = out[:, :LANES] * (1.0 / out[:, LANES:])
                col = j * 256 + pair * LANES
                o_ref[pl.ds(r0, blk), col:col + LANES] = out.astype(o_ref.dtype)

    def even_block(t, has_prev=True, maybe_no_prev=False):
        scores(t + 1, s_odd)
        softmax(s_even, p_even, maybe_no_prev)
        if has_prev:
            values(t - 1, p_odd)

    def odd_block(t, has_next=True):
        if has_next:
            scores(t + 1, s_even)
        softmax(s_odd, p_odd)
        values(t - 1, p_even)

    def block_pair(u, carry):
        t = 2 * u
        even_block(t)
        odd_block(t + 1)
        return carry

    scores(0, s_even)
    even_block(0, has_prev=False, maybe_no_prev=True)
    odd_block(1)
    lax.fori_loop(1, nblk // 2 - 1, block_pair, 0, unroll=True)
    even_block(nblk - 2)
    odd_block(nblk - 1, has_next=False)
    values(nblk - 1, p_odd)


def _attention(proj, sink, bias, seq):
    nt = proj.shape[0]
    tq = min(ATTN_TOKENS, seq)
    per = tq // ATTN_BLOCK
    assert per % 2 == 0 and per >= 4, "the block pipeline needs an even number (>= 4) of blocks per step"
    steps_per_seq = seq // tq
    cq, ck, cv = COL_Q // 1024, COL_K // 256, COL_V // 256

    def prev(i):
        return jnp.maximum(i * per - 1, 0)

    return pl.pallas_call(
        functools.partial(_attn_kernel, steps_per_seq=steps_per_seq),
        grid=(nt // tq,),
        in_specs=[
            pl.BlockSpec(memory_space=pltpu.SMEM),
            pl.BlockSpec((tq, 1024), lambda i: (i, cq)),
            pl.BlockSpec((tq, 256), lambda i: (i, ck)),
            pl.BlockSpec((tq, 256), lambda i: (i, cv)),
            pl.BlockSpec((ATTN_BLOCK, 256), lambda i: (prev(i), ck)),
            pl.BlockSpec((ATTN_BLOCK, 256), lambda i: (prev(i), cv)),
            pl.BlockSpec((N_HEADS, 2 * ATTN_BLOCK, ATTN_BLOCK), lambda i: (0, 0, 0)),
        ],
        out_specs=pl.BlockSpec((tq, 1024), lambda i: (i, 0)),
        out_shape=jax.ShapeDtypeStruct((nt, 1024), jnp.bfloat16),
        scratch_shapes=[pltpu.VMEM((tq + ATTN_BLOCK, 256), jnp.bfloat16),
                        pltpu.VMEM((tq + ATTN_BLOCK, 256), jnp.bfloat16),
                        pltpu.VMEM((N_KV_HEADS, 4 * ATTN_BLOCK, 2 * ATTN_BLOCK), jnp.float32),
                        pltpu.VMEM((N_KV_HEADS, 4 * ATTN_BLOCK, 2 * ATTN_BLOCK), jnp.float32),
                        pltpu.VMEM((N_HEADS, 2 * ATTN_BLOCK, ATTN_BLOCK), jnp.bfloat16),
                        pltpu.VMEM((N_HEADS, 2 * ATTN_BLOCK, ATTN_BLOCK), jnp.bfloat16)],
        compiler_params=_params("arbitrary"),
        name="attention",
    )(sink, proj, proj, proj, proj, proj, bias)


def _conv_kernel(a0_ref, a1_ref, g0_ref, g1_ref, ap0_ref, ap1_ref, gp0_ref, gp1_ref, w_ref, b_ref, lng_ref, lnb_ref,
                 o_ref, y_scr, c_scr, wb_scr, *, steps_per_seq):
    tm = a0_ref.shape[0]
    halo = CONV_HALO
    first_step = pl.program_id(0) % steps_per_seq == 0
    halves = ((a0_ref, g0_ref, ap0_ref, gp0_ref), (a1_ref, g1_ref, ap1_ref, gp1_ref))
    for half, (a_ref, g_ref, ap_ref, gp_ref) in enumerate(halves):
        hs = slice(half * HALF, (half + 1) * HALF)
        yp = ap_ref[...].astype(jnp.float32) * jax.nn.sigmoid(gp_ref[...].astype(jnp.float32))
        y_scr[0, 0:halo, hs] = jnp.where(first_step, 0.0, yp)
        y_scr[0, halo:, hs] = a_ref[...].astype(jnp.float32) * jax.nn.sigmoid(g_ref[...].astype(jnp.float32))
    shifted_rows = tm + halo - SUBLANES
    for r in range(1, SUBLANES):
        y_scr[r, 0:shifted_rows, :] = y_scr[0, r:r + shifted_rows, :]
    base = halo - (CONV_WIDTH - 1)

    @pl.when(pl.program_id(0) == 0)
    def _():
        for j in range(CONV_WIDTH):
            wb_scr[j] = jnp.broadcast_to(w_ref[j:j + 1, :], (SUBLANES, CONV_CH))
        wb_scr[CONV_WIDTH] = jnp.broadcast_to(b_ref[...], (SUBLANES, CONV_CH))

    lane_chunks = CONV_CH // CONV_COLS
    row_groups = CONV_ROWS // SUBLANES

    def tile(t, carry):
        r0 = pl.multiple_of((t // lane_chunks) * CONV_ROWS, CONV_ROWS)
        cs = pl.ds(pl.multiple_of((t % lane_chunks) * CONV_COLS, CONV_COLS), CONV_COLS)
        accs = [wb_scr[CONV_WIDTH, :, cs]] * row_groups
        for phase in range(SUBLANES):
            taps = [j for j in range(CONV_WIDTH) if (base + j) % SUBLANES == phase]
            first = [(base + j) // SUBLANES for j in taps]
            groups = {k: y_scr[phase, pl.ds(r0 + k * SUBLANES, SUBLANES), cs]
                      for k in range(min(first), max(first) + row_groups)}
            for j, k0 in zip(taps, first):
                wj = wb_scr[j, :, cs]
                for g in range(row_groups):
                    accs[g] = accs[g] + groups[k0 + g] * wj
        for g in range(row_groups):
            c_scr[pl.ds(r0 + g * SUBLANES, SUBLANES), cs] = accs[g]
        return carry

    lax.fori_loop(0, (tm // CONV_ROWS) * lane_chunks, tile, 0, unroll=CONV_UNROLL)
    y = c_scr[...]
    mu = jnp.mean(y, axis=-1, keepdims=True)
    yc = y - mu
    z = yc * lax.rsqrt(jnp.mean(yc * yc, axis=-1, keepdims=True) + EPS) * lng_ref[...] + lnb_ref[...]
    o_ref[...] = (z * jax.nn.sigmoid(z)).astype(o_ref.dtype)


def _conv(proj, w, b, ln_g, ln_b, seq):
    nt = proj.shape[0]
    tm = min(CONV_TOKENS, seq)
    steps_per_seq = seq // tm
    per = tm // CONV_HALO
    ca, cg = COL_CONV_A // HALF, COL_CONV_G // HALF

    def prev(i):
        return jnp.maximum(i * per - 1, 0)

    def cur(c):
        return pl.BlockSpec((tm, HALF), lambda i: (i, c))

    def before(c):
        return pl.BlockSpec((CONV_HALO, HALF), lambda i: (prev(i), c))

    vec = pl.BlockSpec((1, CONV_CH), lambda i: (0, 0))
    return pl.pallas_call(
        functools.partial(_conv_kernel, steps_per_seq=steps_per_seq),
        grid=(nt // tm,),
        in_specs=[
            cur(ca), cur(ca + 1), cur(cg), cur(cg + 1),
            before(ca), before(ca + 1), before(cg), before(cg + 1),
            pl.BlockSpec((CONV_WIDTH, CONV_CH), lambda i: (0, 0)),
            vec, vec, vec,
        ],
        out_specs=pl.BlockSpec((tm, CONV_CH), lambda i: (i, 0)),
        out_shape=jax.ShapeDtypeStruct((nt, CONV_CH), jnp.bfloat16),
        scratch_shapes=[pltpu.VMEM((SUBLANES, tm + CONV_HALO, CONV_CH), jnp.float32),
                        pltpu.VMEM((tm, CONV_CH), jnp.float32),
                        pltpu.VMEM((CONV_WIDTH + 1, SUBLANES, CONV_CH), jnp.float32)],
        compiler_params=_params("arbitrary"),
        name="conformer_conv",
    )(*([proj] * 8), w, b, ln_g, ln_b)


def _split_bf16(x):
    hi = x.astype(jnp.bfloat16)
    lo = (x - hi.astype(jnp.float32)).astype(jnp.bfloat16)
    return hi, lo


def _gla_kernel(q_ref, k_ref, v0_ref, v1_ref, g0_ref, g1_ref, lr_ref, w2_ref, gb_ref, ng_ref, o_ref,
                state_t, tri_scr, qe_scr, ke_scr, b_scr, oi_scr, kvt_scr, *, steps_per_seq):
    L = GLA_CHUNK
    tm = q_ref.shape[0]
    nchunk = tm // L

    @pl.when(pl.program_id(0) == 0)
    def _():
        row = lax.broadcasted_iota(jnp.int32, (tm, tm), 0)
        col = lax.broadcasted_iota(jnp.int32, (tm, tm), 1)
        tri_scr[...] = ((row // L == col // L) & (row >= col)).astype(jnp.bfloat16)

    @pl.when(pl.program_id(0) % steps_per_seq == 0)
    def _():
        state_t[...] = jnp.zeros_like(state_t)

    z = _dot(lr_ref[...], w2_ref[...]) + gb_ref[...]
    gk = (jnp.minimum(z, 0.0) - jnp.log(1.0 + jnp.exp(-jnp.abs(z)))) * (1.0 / GLA_GATE_NORM)
    gk_hi, gk_lo = _split_bf16(gk)
    b = _dot(tri_scr[...], gk_hi) + _dot(tri_scr[...], gk_lo)
    b_scr[...] = b
    qe_scr[...] = (q_ref[...].astype(jnp.float32) * (GLA_DK ** -0.5) * jnp.exp(b)).astype(jnp.bfloat16)
    ke_scr[...] = (k_ref[...].astype(jnp.float32) * jnp.exp(-b)).astype(jnp.bfloat16)

    causal = lax.broadcasted_iota(jnp.int32, (L, L), 0) >= lax.broadcasted_iota(jnp.int32, (L, L), 1)

    def head_cols(halves, rs, h):
        per_half = HALF // GLA_DV
        return halves[h // per_half][rs, (h % per_half) * GLA_DV:(h % per_half + 1) * GLA_DV]

    def local(c, carry):
        rs = pl.ds(pl.multiple_of(c * L, L), L)
        heads = [(slice(h * GLA_DK, (h + 1) * GLA_DK), slice(h * GLA_DV, (h + 1) * GLA_DV)) for h in range(GLA_HEADS)]
        scores = [_dot_nt(qe_scr[rs, ks], ke_scr[rs, ks]) for ks, _ in heads]
        for h, (ks, vs) in enumerate(heads):
            kvt_scr[c, h] = _dot_tn(head_cols((v0_ref, v1_ref), rs, h), ke_scr[rs, ks])
        atts = [jnp.where(causal, s, 0.0).astype(jnp.bfloat16) for s in scores]
        for h, (_, vs) in enumerate(heads):
            oi_scr[rs, vs] = _dot(atts[h], head_cols((v0_ref, v1_ref), rs, h))
        return carry

    lax.fori_loop(0, nchunk, local, 0, unroll=GLA_UNROLL)

    def recur(c, carry):
        r0 = pl.multiple_of(c * L, L)
        rs = pl.ds(r0, L)
        b_tail = b_scr[pl.ds(r0 + L - SUBLANES, SUBLANES), :]
        for h in range(GLA_HEADS):
            ks = slice(h * GLA_DK, (h + 1) * GLA_DK)
            vs = slice(h * GLA_DV, (h + 1) * GLA_DV)
            s_old = state_t[h]
            o = oi_scr[rs, vs] + _dot_nt(qe_scr[rs, ks], s_old.astype(jnp.bfloat16))
            decay = jnp.exp(b_tail[SUBLANES - 1:SUBLANES, ks])
            state_t[h] = decay * (s_old + kvt_scr[c, h])
            o = _rms(o, ng_ref[...])
            gate = head_cols((g0_ref, g1_ref), rs, h).astype(jnp.float32)
            o_ref[rs, vs] = (o * (gate * jax.nn.sigmoid(gate))).astype(o_ref.dtype)
        return carry

    lax.fori_loop(0, nchunk, recur, 0, unroll=GLA_UNROLL)


def _gla(proj, w2, gate_b, norm_g, seq):
    nt = proj.shape[0]
    tm = min(GLA_TOKENS, seq)
    steps_per_seq = seq // tm
    return pl.pallas_call(
        functools.partial(_gla_kernel, steps_per_seq=steps_per_seq),
        grid=(nt // tm,),
        in_specs=[
            pl.BlockSpec((tm, 512), lambda i: (i, COL_GLA_Q // 512)),
            pl.BlockSpec((tm, 512), lambda i: (i, COL_GLA_K // 512)),
            pl.BlockSpec((tm, HALF), lambda i: (i, COL_GLA_V // HALF)),
            pl.BlockSpec((tm, HALF), lambda i: (i, COL_GLA_V // HALF + 1)),
            pl.BlockSpec((tm, HALF), lambda i: (i, COL_GLA_G // HALF)),
            pl.BlockSpec((tm, HALF), lambda i: (i, COL_GLA_G // HALF + 1)),
            pl.BlockSpec((tm, LANES), lambda i: (i, COL_LR // LANES)),
            pl.BlockSpec((LANES, 512), lambda i: (0, 0)),
            pl.BlockSpec((1, 512), lambda i: (0, 0)),
            pl.BlockSpec((1, GLA_DV), lambda i: (0, 0)),
        ],
        out_specs=pl.BlockSpec((tm, 1024), lambda i: (i, 0)),
        out_shape=jax.ShapeDtypeStruct((nt, 1024), jnp.bfloat16),
        scratch_shapes=[
            pltpu.VMEM((GLA_HEADS, GLA_DV, GLA_DK), jnp.float32),
            pltpu.VMEM((tm, tm), jnp.bfloat16),
            pltpu.VMEM((tm, GLA_HEADS * GLA_DK), jnp.bfloat16),
            pltpu.VMEM((tm, GLA_HEADS * GLA_DK), jnp.bfloat16),
            pltpu.VMEM((tm, GLA_HEADS * GLA_DK), jnp.float32),
            pltpu.VMEM((tm, GLA_HEADS * GLA_DV), jnp.float32),
            pltpu.VMEM((tm // GLA_CHUNK, GLA_HEADS, GLA_DV, GLA_DK), jnp.float32),
        ],
        compiler_params=_params("arbitrary"),
        name="gla",
    )(*([proj] * 7), w2, gate_b, norm_g)


def _merge_kernel(x_ref, ya_ref, yb_ref, yc_ref, ga_ref, gb_ref, gc_ref, wa_ref, wb_ref, wc_ref, wo_ref, g_ref, o_ref,
                  m_scr, *, nb):
    t = pl.program_id(0)

    @pl.when(t == 0)
    def _():
        m_scr[1] = jnp.zeros(m_scr.shape[1:], m_scr.dtype)

    def branch(gate_ref, y_ref, w_ref):
        return jax.nn.sigmoid(gate_ref[...].astype(jnp.float32)) * _dot(y_ref[...], w_ref[...])

    def matmuls(slot):
        merged = branch(ga_ref, ya_ref, wa_ref) + branch(gb_ref, yb_ref, wb_ref) + branch(gc_ref, yc_ref, wc_ref)
        m_scr[slot] = _dot(merged.astype(jnp.bfloat16), wo_ref[...])

    def finish(slot):
        o_ref[...] = x_ref[...] + _rms(m_scr[slot], g_ref[...])

    for parity in (0, 1):
        @pl.when((t < nb) & (t % 2 == parity))
        def _():
            finish(1 - parity)
            matmuls(parity)

    @pl.when(t == nb)
    def _():
        finish((nb - 1) % 2)


def _merge(x, ya, yb, yc, proj, wa, wb, wc, wo, g, l):
    nt, d = x.shape
    bm = min(MERGE_BLOCK_M, nt)
    nb = nt // bm
    c0 = COL_GATES // d

    def resident(shape):
        return pl.BlockSpec((None,) + shape, lambda t: (l, 0, 0), pipeline_mode=pl.Buffered(1))

    def cur(t):
        return jnp.minimum(t, nb - 1)

    def before(t):
        return jnp.maximum(t - 1, 0)

    def branch_in():
        return pl.BlockSpec((bm, 1024), lambda t: (cur(t), 0))

    return pl.pallas_call(
        functools.partial(_merge_kernel, nb=nb),
        grid=(nb + 1,),
        in_specs=[
            pl.BlockSpec((bm, d), lambda t: (before(t), 0)),
            branch_in(), branch_in(), branch_in(),
            pl.BlockSpec((bm, d), lambda t: (cur(t), c0)),
            pl.BlockSpec((bm, d), lambda t: (cur(t), c0 + 1)),
            pl.BlockSpec((bm, d), lambda t: (cur(t), c0 + 2)),
            resident((1024, d)), resident((1024, d)), resident((1024, d)), resident((d, d)),
            pl.BlockSpec((1, d), lambda t: (0, 0)),
        ],
        out_specs=pl.BlockSpec((bm, d), lambda t: (before(t), 0)),
        out_shape=jax.ShapeDtypeStruct((nt, d), jnp.float32),
        scratch_shapes=[pltpu.VMEM((2, bm, d), jnp.float32)],
        compiler_params=_params("arbitrary"),
        name="merge",
    )(x, ya, yb, yc, proj, proj, proj, wa, wb, wc, wo, g)


def _split_w_in(w_in):
    gates = w_in[..., MAIN_COLS:].astype(jnp.bfloat16)
    main = jnp.pad(w_in[..., :MAIN_COLS].astype(jnp.bfloat16), ((0, 0), (0, 0), (0, MAIN_WIDTH - MAIN_COLS)))
    return gates, main


def kernel(x, rel_bias, ffn1_pre_g, ffn1_post_g, ffn1_w_gate, ffn1_w_up, ffn1_w_down, mix_pre_g, mix_post_g, w_in, attn_sink, conv_w, conv_b, conv_ln_g, conv_ln_b, gla_gate_w2, gla_gate_b, gla_norm_g, w_a_up, w_b_up, w_c_up, w_out, ffn2_pre_g, ffn2_post_g, ffn2_w_gate, ffn2_w_up, ffn2_w_down):
    batch, seq, d = x.shape
    depth = w_in.shape[0]
    bf = lambda w: w.astype(jnp.bfloat16)
    row = lambda v: v.reshape(1, -1)
    bias = _band_bias(rel_bias)
    f1g, f1u, f1d = bf(ffn1_w_gate), bf(ffn1_w_up), bf(ffn1_w_down)
    f2g, f2u, f2d = bf(ffn2_w_gate), bf(ffn2_w_up), bf(ffn2_w_down)
    w_gates, w_main = _split_w_in(w_in)
    wa, wb, wc, wo = bf(w_a_up), bf(w_b_up), bf(w_c_up), bf(w_out)
    w2 = jnp.pad(bf(gla_gate_w2), ((0, 0), (0, LANES - GLA_RANK), (0, 0)))
    xt = x.reshape(batch * seq, d)
    for l in range(depth):
        xt, h_mix = _ffn(xt, row(ffn1_pre_g[l]), row(ffn1_post_g[l]), f1g, f1u, f1d, l, gnext=row(mix_pre_g[l]))
        proj = _proj(h_mix, w_gates, w_main, l)
        ya = _attention(proj, attn_sink[l], bias, seq)
        yb = _conv(proj, conv_w[l], row(conv_b[l]), row(conv_ln_g[l]), row(conv_ln_b[l]), seq)
        yc = _gla(proj, w2[l], row(gla_gate_b[l]), row(gla_norm_g[l]), seq)
        xt = _merge(xt, ya, yb, yc, proj, wa, wb, wc, wo, row(mix_post_g[l]), l)
        xt = _ffn(xt, row(ffn2_pre_g[l]), row(ffn2_post_g[l]), f2g, f2u, f2d, l)
    return xt.reshape(batch, seq, d)
```

```python
import functools
import math

import jax
import jax.numpy as jnp
from jax import lax
from jax.experimental import pallas as pl
from jax.experimental.pallas import tpu as pltpu

EPS = 1e-6
D_MODEL = 2048
D_FF = 5632
HEAD_DIM = 64
N_HEADS = 16
N_KV_HEADS = 4
ATTN_BLOCK = 128
REL_BUCKETS = 32
REL_MAX_DIST = 128
CONV_CH = 1024
CONV_WIDTH = 31
GLA_HEADS = 4
GLA_DK = 128
GLA_DV = 256
GLA_CHUNK = 64
GLA_RANK = 16
GLA_GATE_NORM = 16.0
MASK_VALUE = -1e30

LANES = 128
SUBLANES = 8
VMEM_LIMIT_BYTES = 56 * 1024 * 1024

COL_GATES = 0
COL_MAIN = 3 * D_MODEL
COL_Q = COL_MAIN
COL_K = COL_MAIN + 1024
COL_V = COL_MAIN + 1280
COL_CONV_A = COL_MAIN + 1536
COL_CONV_G = COL_MAIN + 2560
COL_GLA_Q = COL_MAIN + 3584
COL_GLA_K = COL_MAIN + 4096
COL_GLA_V = COL_MAIN + 4608
COL_GLA_G = COL_MAIN + 5632
COL_LR = COL_MAIN + 6656
MAIN_COLS = 6672
MAIN_WIDTH = 7168
HALF = 512
PROJ_BLOCK_N = 1024

FFN_BLOCK_M = 1024
FFN_BLOCK_F = 512
FFN_SPLIT = 2
FFN_SLICES = 8
FFN_SLICE_ALIGN = 16
PROJ_BLOCK_M = 2048
ATTN_TOKENS = 1024
CONV_TOKENS = 512
CONV_HALO = 32
CONV_ROWS = 32
CONV_COLS = 512
CONV_UNROLL = 4
GLA_TOKENS = 512
GLA_UNROLL = 8
MERGE_BLOCK_M = 256


def _params(*semantics):
    return pltpu.CompilerParams(dimension_semantics=semantics, vmem_limit_bytes=VMEM_LIMIT_BYTES)


def _rms(x, g):
    return x * lax.rsqrt(jnp.mean(x * x, axis=-1, keepdims=True) + EPS) * g


def _dot(a, b):
    return jnp.dot(a, b, preferred_element_type=jnp.float32)


def _dot_nt(a, b):
    return lax.dot_general(a, b, (((1,), (1,)), ((), ())), preferred_element_type=jnp.float32)


def _dot_tn(a, b):
    return lax.dot_general(a, b, (((0,), (0,)), ((), ())), preferred_element_type=jnp.float32)


def _ffn_kernel(*refs, nb, emit_next):
    if emit_next:
        (xp_ref, xn_ref, gpre_ref, gpost_ref, gnext_ref, wg_ref, wu_ref, wd_ref, o_ref, hn_ref,
         h_even, h_odd, acc_even, acc_odd) = refs
    else:
        xp_ref, xn_ref, gpre_ref, gpost_ref, wg_ref, wu_ref, wd_ref, o_ref, h_even, h_odd, acc_even, acc_odd = refs
    _ffn_body(xp_ref, xn_ref, gpre_ref, gpost_ref, wg_ref, wu_ref, wd_ref, o_ref, h_even, h_odd, acc_even, acc_odd,
              (gnext_ref, hn_ref) if emit_next else None, nb)


def _ffn_body(xp_ref, xn_ref, gpre_ref, gpost_ref, wg_ref, wu_ref, wd_ref, o_ref, h_even, h_odd, acc_even, acc_odd,
              next_refs, nb):
    h_scr, acc_scr = (h_even, h_odd), (acc_even, acc_odd)
    r, f = pl.program_id(0), pl.program_id(1)
    rows = pl.ds(pl.multiple_of(jnp.minimum(f, FFN_SLICES - 1) * xn_ref.shape[0], xn_ref.shape[0]), xn_ref.shape[0])

    @pl.when((r == 0) & (f == 0))
    def _():
        acc_even[...] = jnp.zeros_like(acc_even)
        acc_odd[...] = jnp.zeros_like(acc_odd)

    def finish(slot):
        y = xp_ref[...] + 0.5 * _rms(acc_scr[slot][rows, :], gpost_ref[...])
        o_ref[...] = y
        if next_refs is not None:
            gnext_ref, hn_ref = next_refs
            hn_ref[...] = _rms(y, gnext_ref[...]).astype(hn_ref.dtype)

    def prepare(slot):
        h_scr[slot][rows, :] = _rms(xn_ref[...], gpre_ref[...]).astype(jnp.bfloat16)

    def up(slot):
        h = h_scr[slot][...]
        acts = []
        for s in range(FFN_SPLIT):
            cols = slice(s * wg_ref.shape[1] // FFN_SPLIT, (s + 1) * wg_ref.shape[1] // FFN_SPLIT)
            a = _dot(h, wg_ref[:, cols])
            u = _dot(h, wu_ref[:, cols])
            acts.append((a * jax.nn.sigmoid(a) * u).astype(jnp.bfloat16))
        return jnp.concatenate(acts, axis=1)

    def down(slot, act):
        carried = jnp.where(f > 0, acc_scr[slot][...], 0.0)
        acc_scr[slot][...] = carried + _dot(act, wd_ref[...])

    @pl.when(r == 0)
    def _():
        prepare(0)

    for parity in (0, 1):
        @pl.when((r >= 1) & (r <= nb) & ((r - 1) % 2 == parity))
        def _():
            finish(1 - parity)
            prepare(1 - parity)
            down(parity, up(parity))

    @pl.when(r == nb + 1)
    def _():
        finish((nb - 1) % 2)


def _ffn(x, gpre, gpost, wg, wu, wd, l, gnext=None):
    nt, d = x.shape
    ff = wg.shape[2]
    bm = min(FFN_BLOCK_M, nt)
    bf = FFN_BLOCK_F
    nb, nf = nt // bm, ff // bf
    rs = bm // FFN_SLICES
    assert nf >= FFN_SLICES and rs % FFN_SLICE_ALIGN == 0

    def row_slice(block, f):
        return jnp.clip(block, 0, nb - 1) * FFN_SLICES + jnp.minimum(f, FFN_SLICES - 1)

    def chunk(r, f):
        return jnp.where(r == 0, 0, jnp.where(r == nb + 1, nf - 1, f))

    emit_next = gnext is not None
    vec = pl.BlockSpec((1, d), lambda r, f: (0, 0))
    out_spec = pl.BlockSpec((rs, d), lambda r, f: (jnp.where(r < 2, 0, row_slice(r - 2, f)), 0))
    out = pl.pallas_call(
        functools.partial(_ffn_kernel, nb=nb, emit_next=emit_next),
        grid=(nb + 2, nf),
        in_specs=[
            pl.BlockSpec((rs, d), lambda r, f: (row_slice(r - 2, f), 0)),
            pl.BlockSpec((rs, d), lambda r, f: (row_slice(r, f), 0)),
            vec, vec, *([vec] if emit_next else []),
            pl.BlockSpec((None, d, bf), lambda r, f: (l, 0, chunk(r, f))),
            pl.BlockSpec((None, d, bf), lambda r, f: (l, 0, chunk(r, f))),
            pl.BlockSpec((None, bf, d), lambda r, f: (l, chunk(r, f), 0)),
        ],
        out_specs=[out_spec, out_spec] if emit_next else out_spec,
        out_shape=([jax.ShapeDtypeStruct((nt, d), jnp.float32), jax.ShapeDtypeStruct((nt, d), jnp.bfloat16)]
                   if emit_next else jax.ShapeDtypeStruct((nt, d), jnp.float32)),
        scratch_shapes=[pltpu.VMEM((bm, d), jnp.bfloat16), pltpu.VMEM((bm, d), jnp.bfloat16),
                        pltpu.VMEM((bm, d), jnp.float32), pltpu.VMEM((bm, d), jnp.float32)],
        compiler_params=_params("arbitrary", "arbitrary"),
        name="ffn",
    )(x, x, gpre, gpost, *([gnext] if emit_next else []), wg, wu, wd)
    return tuple(out) if emit_next else out


def _proj_kernel(h_ref, wg_ref, wm_ref, o_ref, *, gate_blocks):
    j = pl.program_id(1)

    @pl.when(j < gate_blocks)
    def _():
        o_ref[...] = _dot(h_ref[...], wg_ref[...]).astype(o_ref.dtype)

    @pl.when(j >= gate_blocks)
    def _():
        o_ref[...] = _dot(h_ref[...], wm_ref[...]).astype(o_ref.dtype)


def _proj(h, w_gates, w_main, l):
    nt, d = h.shape
    bm = min(PROJ_BLOCK_M, nt)
    bn = PROJ_BLOCK_N
    gate_blocks, main_blocks = w_gates.shape[2] // bn, w_main.shape[2] // bn
    return pl.pallas_call(
        functools.partial(_proj_kernel, gate_blocks=gate_blocks),
        grid=(nt // bm, gate_blocks + main_blocks),
        in_specs=[
            pl.BlockSpec((bm, d), lambda i, j: (i, 0)),
            pl.BlockSpec((None, d, bn), lambda i, j: (l, 0, jnp.minimum(j, gate_blocks - 1))),
            pl.BlockSpec((None, d, bn), lambda i, j: (
                l, 0, jnp.where(j < gate_blocks - 1, main_blocks - 1, jnp.maximum(j - gate_blocks, 0)))),
        ],
        out_specs=pl.BlockSpec((bm, bn), lambda i, j: (i, j)),
        out_shape=jax.ShapeDtypeStruct((nt, (gate_blocks + main_blocks) * bn), jnp.bfloat16),
        compiler_params=_params("parallel", "arbitrary"),
        name="proj",
    )(h, w_gates, w_main)


def _rel_buckets():
    qi = jnp.arange(ATTN_BLOCK)[None, :]
    kj = jnp.arange(2 * ATTN_BLOCK)[:, None]
    n = jnp.maximum(ATTN_BLOCK + qi - kj, 0)
    max_exact = REL_BUCKETS // 2
    nf = jnp.maximum(n, 1).astype(jnp.float32)
    large = max_exact + (jnp.log(nf / max_exact) / math.log(REL_MAX_DIST / max_exact)
                         * (REL_BUCKETS - max_exact)).astype(jnp.int32)
    large = jnp.minimum(large, REL_BUCKETS - 1)
    return jnp.where(n < max_exact, n, large).astype(jnp.int32)


def _bias_kernel(rel_ref, bucket_ref, o_ref):
    h = pl.program_id(0)
    bucket = bucket_ref[...]
    kj = lax.broadcasted_iota(jnp.int32, bucket.shape, 0)
    qi = lax.broadcasted_iota(jnp.int32, bucket.shape, 1)
    dist = ATTN_BLOCK + qi - kj
    acc = jnp.zeros(bucket.shape, jnp.float32)
    for b in range(REL_BUCKETS):
        acc = jnp.where(bucket == b, rel_ref[b, h], acc)
    o_ref[0] = jnp.where((dist >= 0) & (dist < ATTN_BLOCK), acc, MASK_VALUE)


def _band_bias(rel_bias):
    shape = (2 * ATTN_BLOCK, ATTN_BLOCK)
    return pl.pallas_call(
        _bias_kernel,
        grid=(N_HEADS,),
        in_specs=[
            pl.BlockSpec(memory_space=pltpu.SMEM),
            pl.BlockSpec(shape, lambda h: (0, 0)),
        ],
        out_specs=pl.BlockSpec((1,) + shape, lambda h: (h, 0, 0)),
        out_shape=jax.ShapeDtypeStruct((N_HEADS,) + shape, jnp.float32),
        compiler_params=_params("arbitrary"),
        name="band_bias",
    )(rel_bias, _rel_buckets())


def _lane_swap_halves(x):
    y = pltpu.roll(pltpu.bitcast(x, jnp.uint32), HEAD_DIM, axis=1)
    return pltpu.bitcast(y, jnp.bfloat16)


def _head_pair_tiles(t128, head_in_high_lanes):
    lane = lax.broadcasted_iota(jnp.int32, t128.shape, 1)
    zero = jnp.zeros_like(t128)
    if head_in_high_lanes:
        hi = jnp.where(lane >= HEAD_DIM, t128, zero)
        return _lane_swap_halves(hi), hi
    lo = jnp.where(lane < HEAD_DIM, t128, zero)
    return lo, _lane_swap_halves(lo)


def _attn_kernel(sink_ref, q_ref, k_ref, v_ref, kp_ref, vp_ref, bias_ref, o_ref, kfull, vfull, s_even, s_odd, p_even, p_odd,
                 *, steps_per_seq):
    blk = ATTN_BLOCK
    nblk = q_ref.shape[0] // blk
    first_step = pl.program_id(0) % steps_per_seq == 0
    kfull[0:blk, :] = kp_ref[...]
    vfull[0:blk, :] = vp_ref[...]
    kfull[blk:, :] = k_ref[...]
    vfull[blk:, :] = v_ref[...]

    def first_row(n):
        return n * blk if isinstance(n, int) else pl.multiple_of(n * blk, blk)

    def scores(n, s_scr):
        r0 = first_row(n)
        kband = kfull[pl.ds(r0, 2 * blk), :]
        for j in range(N_KV_HEADS):
            c = (j // 2) * LANES
            ka, kb = _head_pair_tiles(kband[:, c:c + LANES], j % 2 == 1)
            q0 = q_ref[pl.ds(r0, blk), j * 256:j * 256 + LANES]
            q1 = q_ref[pl.ds(r0, blk), j * 256 + LANES:(j + 1) * 256]
            qs = jnp.concatenate([q0, q1], axis=0) * (HEAD_DIM ** -0.5)
            s_scr[j] = _dot_nt(jnp.concatenate([ka, kb], axis=0), qs)

    def softmax(s_scr, p_scr, maybe_no_prev=False):
        for h in range(N_HEADS):
            j, g = h // 4, h % 4
            logits = s_scr[j, (g % 2) * 2 * blk:(g % 2 + 1) * 2 * blk, (g // 2) * blk:(g // 2 + 1) * blk]
            logits = logits + bias_ref[h]
            if maybe_no_prev:
                prev_rows = jnp.where(first_step, MASK_VALUE, logits[:blk])
                logits = jnp.concatenate([prev_rows, logits[blk:]], axis=0)
            sink = sink_ref[h]
            m = jnp.maximum(jnp.max(logits, axis=0, keepdims=True), sink)
            p = jnp.exp(logits - m)
            key0 = lax.broadcasted_iota(jnp.int32, (SUBLANES, blk), 0) == 0
            top = jnp.where(key0, jnp.exp(sink - m), p[:SUBLANES])
            p_scr[h] = jnp.concatenate([top, p[SUBLANES:]], axis=0).astype(jnp.bfloat16)

    lane = lax.broadcasted_iota(jnp.int32, (2 * blk, LANES), 1)
    ones_even = (lane < HEAD_DIM).astype(jnp.bfloat16)
    ones_odd = (lane >= HEAD_DIM).astype(jnp.bfloat16)
    not_key0 = lax.broadcasted_iota(jnp.int32, (2 * blk, LANES), 0) > 0

    def values(n, p_scr):
        r0 = first_row(n)
        vband = vfull[pl.ds(r0, 2 * blk), :]
        for j in range(N_KV_HEADS):
            c = (j // 2) * LANES
            v_tile = jnp.where(not_key0, vband[:, c:c + LANES], jnp.zeros((2 * blk, LANES), jnp.bfloat16))
            va, vb = _head_pair_tiles(v_tile, j % 2 == 1)
            rhs_even = jnp.concatenate([va, ones_even], axis=1)
            rhs_odd = jnp.concatenate([vb, ones_odd], axis=1)
            for pair in range(2):
                h = j * 4 + 2 * pair
                out = _dot_tn(p_scr[h], rhs_even) + _dot_tn(p_scr[h + 1], rhs_odd)
                out = out[:, :LANES] * (1.0 / out[:, LANES:])
                col = j * 256 + pair * LANES
                o_ref[pl.ds(r0, blk), col:col + LANES] = out.astype(o_ref.dtype)

    def even_block(t, has_prev=True, maybe_no_prev=False):
        scores(t + 1, s_odd)
        softmax(s_even, p_even, maybe_no_prev)
        if has_prev:
            values(t - 1, p_odd)

    def odd_block(t, has_next=True):
        if has_next:
            scores(t + 1, s_even)
        softmax(s_odd, p_odd)
        values(t - 1, p_even)

    def block_pair(u, carry):
        t = 2 * u
        even_block(t)
        odd_block(t + 1)
        return carry

    scores(0, s_even)
    even_block(0, has_prev=False, maybe_no_prev=True)
    odd_block(1)
    lax.fori_loop(1, nblk // 2 - 1, block_pair, 0, unroll=True)
    even_block(nblk - 2)
    odd_block(nblk - 1, has_next=False)
    values(nblk - 1, p_odd)


def _attention(proj, sink, bias, seq):
    nt = proj.shape[0]
    tq = min(ATTN_TOKENS, seq)
    per = tq // ATTN_BLOCK
    assert per % 2 == 0 and per >= 4, "the block pipeline needs an even number (>= 4) of blocks per step"
    steps_per_seq = seq // tq
    cq, ck, cv = COL_Q // 1024, COL_K // 256, COL_V // 256

    def prev(i):
        return jnp.maximum(i * per - 1, 0)

    return pl.pallas_call(
        functools.partial(_attn_kernel, steps_per_seq=steps_per_seq),
        grid=(nt // tq,),
        in_specs=[
            pl.BlockSpec(memory_space=pltpu.SMEM),
            pl.BlockSpec((tq, 1024), lambda i: (i, cq)),
            pl.BlockSpec((tq, 256), lambda i: (i, ck)),
            pl.BlockSpec((tq, 256), lambda i: (i, cv)),
            pl.BlockSpec((ATTN_BLOCK, 256), lambda i: (prev(i), ck)),
            pl.BlockSpec((ATTN_BLOCK, 256), lambda i: (prev(i), cv)),
            pl.BlockSpec((N_HEADS, 2 * ATTN_BLOCK, ATTN_BLOCK), lambda i: (0, 0, 0)),
        ],
        out_specs=pl.BlockSpec((tq, 1024), lambda i: (i, 0)),
        out_shape=jax.ShapeDtypeStruct((nt, 1024), jnp.bfloat16),
        scratch_shapes=[pltpu.VMEM((tq + ATTN_BLOCK, 256), jnp.bfloat16),
                        pltpu.VMEM((tq + ATTN_BLOCK, 256), jnp.bfloat16),
                        pltpu.VMEM((N_KV_HEADS, 4 * ATTN_BLOCK, 2 * ATTN_BLOCK), jnp.float32),
                        pltpu.VMEM((N_KV_HEADS, 4 * ATTN_BLOCK, 2 * ATTN_BLOCK), jnp.float32),
                        pltpu.VMEM((N_HEADS, 2 * ATTN_BLOCK, ATTN_BLOCK), jnp.bfloat16),
                        pltpu.VMEM((N_HEADS, 2 * ATTN_BLOCK, ATTN_BLOCK), jnp.bfloat16)],
        compiler_params=_params("arbitrary"),
        name="attention",
    )(sink, proj, proj, proj, proj, proj, bias)


def _conv_kernel(a0_ref, a1_ref, g0_ref, g1_ref, ap0_ref, ap1_ref, gp0_ref, gp1_ref, w_ref, b_ref, lng_ref, lnb_ref,
                 o_ref, y_scr, c_scr, wb_scr, *, steps_per_seq):
    tm = a0_ref.shape[0]
    halo = CONV_HALO
    first_step = pl.program_id(0) % steps_per_seq == 0
    halves = ((a0_ref, g0_ref, ap0_ref, gp0_ref), (a1_ref, g1_ref, ap1_ref, gp1_ref))
    for half, (a_ref, g_ref, ap_ref, gp_ref) in enumerate(halves):
        hs = slice(half * HALF, (half + 1) * HALF)
        yp = ap_ref[...].astype(jnp.float32) * jax.nn.sigmoid(gp_ref[...].astype(jnp.float32))
        y_scr[0, 0:halo, hs] = jnp.where(first_step, 0.0, yp)
        y_scr[0, halo:, hs] = a_ref[...].astype(jnp.float32) * jax.nn.sigmoid(g_ref[...].astype(jnp.float32))
    shifted_rows = tm + halo - SUBLANES
    for r in range(1, SUBLANES):
        y_scr[r, 0:shifted_rows, :] = y_scr[0, r:r + shifted_rows, :]
    base = halo - (CONV_WIDTH - 1)

    @pl.when(pl.program_id(0) == 0)
    def _():
        for j in range(CONV_WIDTH):
            wb_scr[j] = jnp.broadcast_to(w_ref[j:j + 1, :], (SUBLANES, CONV_CH))
        wb_scr[CONV_WIDTH] = jnp.broadcast_to(b_ref[...], (SUBLANES, CONV_CH))

    lane_chunks = CONV_CH // CONV_COLS
    row_groups = CONV_ROWS // SUBLANES

    def tile(t, carry):
        r0 = pl.multiple_of((t // lane_chunks) * CONV_ROWS, CONV_ROWS)
        cs = pl.ds(pl.multiple_of((t % lane_chunks) * CONV_COLS, CONV_COLS), CONV_COLS)
        accs = [wb_scr[CONV_WIDTH, :, cs]] * row_groups
        for phase in range(SUBLANES):
            taps = [j for j in range(CONV_WIDTH) if (base + j) % SUBLANES == phase]
            first = [(base + j) // SUBLANES for j in taps]
            groups = {k: y_scr[phase, pl.ds(r0 + k * SUBLANES, SUBLANES), cs]
                      for k in range(min(first), max(first) + row_groups)}
            for j, k0 in zip(taps, first):
                wj = wb_scr[j, :, cs]
                for g in range(row_groups):
                    accs[g] = accs[g] + groups[k0 + g] * wj
        for g in range(row_groups):
            c_scr[pl.ds(r0 + g * SUBLANES, SUBLANES), cs] = accs[g]
        return carry

    lax.fori_loop(0, (tm // CONV_ROWS) * lane_chunks, tile, 0, unroll=CONV_UNROLL)
    y = c_scr[...]
    mu = jnp.mean(y, axis=-1, keepdims=True)
    yc = y - mu
    z = yc * lax.rsqrt(jnp.mean(yc * yc, axis=-1, keepdims=True) + EPS) * lng_ref[...] + lnb_ref[...]
    o_ref[...] = (z * jax.nn.sigmoid(z)).astype(o_ref.dtype)


def _conv(proj, w, b, ln_g, ln_b, seq):
    nt = proj.shape[0]
    tm = min(CONV_TOKENS, seq)
    steps_per_seq = seq // tm
    per = tm // CONV_HALO
    ca, cg = COL_CONV_A // HALF, COL_CONV_G // HALF

    def prev(i):
        return jnp.maximum(i * per - 1, 0)

    def cur(c):
        return pl.BlockSpec((tm, HALF), lambda i: (i, c))

    def before(c):
        return pl.BlockSpec((CONV_HALO, HALF), lambda i: (prev(i), c))

    vec = pl.BlockSpec((1, CONV_CH), lambda i: (0, 0))
    return pl.pallas_call(
        functools.partial(_conv_kernel, steps_per_seq=steps_per_seq),
        grid=(nt // tm,),
        in_specs=[
            cur(ca), cur(ca + 1), cur(cg), cur(cg + 1),
            before(ca), before(ca + 1), before(cg), before(cg + 1),
            pl.BlockSpec((CONV_WIDTH, CONV_CH), lambda i: (0, 0)),
            vec, vec, vec,
        ],
        out_specs=pl.BlockSpec((tm, CONV_CH), lambda i: (i, 0)),
        out_shape=jax.ShapeDtypeStruct((nt, CONV_CH), jnp.bfloat16),
        scratch_shapes=[pltpu.VMEM((SUBLANES, tm + CONV_HALO, CONV_CH), jnp.float32),
                        pltpu.VMEM((tm, CONV_CH), jnp.float32),
                        pltpu.VMEM((CONV_WIDTH + 1, SUBLANES, CONV_CH), jnp.float32)],
        compiler_params=_params("arbitrary"),
        name="conformer_conv",
    )(*([proj] * 8), w, b, ln_g, ln_b)


def _split_bf16(x):
    hi = x.astype(jnp.bfloat16)
    lo = (x - hi.astype(jnp.float32)).astype(jnp.bfloat16)
    return hi, lo


def _gla_kernel(q_ref, k_ref, v0_ref, v1_ref, g0_ref, g1_ref, lr_ref, w2_ref, gb_ref, ng_ref, o_ref,
                state_t, tri_scr, qe_scr, ke_scr, b_scr, oi_scr, kvt_scr, *, steps_per_seq):
    L = GLA_CHUNK
    tm = q_ref.shape[0]
    nchunk = tm // L

    @pl.when(pl.program_id(0) == 0)
    def _():
        row = lax.broadcasted_iota(jnp.int32, (tm, tm), 0)
        col = lax.broadcasted_iota(jnp.int32, (tm, tm), 1)
        tri_scr[...] = ((row // L == col // L) & (row >= col)).astype(jnp.bfloat16)

    @pl.when(pl.program_id(0) % steps_per_seq == 0)
    def _():
        state_t[...] = jnp.zeros_like(state_t)

    z = _dot(lr_ref[...], w2_ref[...]) + gb_ref[...]
    gk = (jnp.minimum(z, 0.0) - jnp.log(1.0 + jnp.exp(-jnp.abs(z)))) * (1.0 / GLA_GATE_NORM)
    gk_hi, gk_lo = _split_bf16(gk)
    b = _dot(tri_scr[...], gk_hi) + _dot(tri_scr[...], gk_lo)
    b_scr[...] = b
    qe_scr[...] = (q_ref[...].astype(jnp.float32) * (GLA_DK ** -0.5) * jnp.exp(b)).astype(jnp.bfloat16)
    ke_scr[...] = (k_ref[...].astype(jnp.float32) * jnp.exp(-b)).astype(jnp.bfloat16)

    causal = lax.broadcasted_iota(jnp.int32, (L, L), 0) >= lax.broadcasted_iota(jnp.int32, (L, L), 1)

    def head_cols(halves, rs, h):
        per_half = HALF // GLA_DV
        return halves[h // per_half][rs, (h % per_half) * GLA_DV:(h % per_half + 1) * GLA_DV]

    def local(c, carry):
        rs = pl.ds(pl.multiple_of(c * L, L), L)
        heads = [(slice(h * GLA_DK, (h + 1) * GLA_DK), slice(h * GLA_DV, (h + 1) * GLA_DV)) for h in range(GLA_HEADS)]
        scores = [_dot_nt(qe_scr[rs, ks], ke_scr[rs, ks]) for ks, _ in heads]
        for h, (ks, vs) in enumerate(heads):
            kvt_scr[c, h] = _dot_tn(head_cols((v0_ref, v1_ref), rs, h), ke_scr[rs, ks])
        atts = [jnp.where(causal, s, 0.0).astype(jnp.bfloat16) for s in scores]
        for h, (_, vs) in enumerate(heads):
            oi_scr[rs, vs] = _dot(atts[h], head_cols((v0_ref, v1_ref), rs, h))
        return carry

    lax.fori_loop(0, nchunk, local, 0, unroll=GLA_UNROLL)

    def recur(c, carry):
        r0 = pl.multiple_of(c * L, L)
        rs = pl.ds(r0, L)
        b_tail = b_scr[pl.ds(r0 + L - SUBLANES, SUBLANES), :]
        for h in range(GLA_HEADS):
            ks = slice(h * GLA_DK, (h + 1) * GLA_DK)
            vs = slice(h * GLA_DV, (h + 1) * GLA_DV)
            s_old = state_t[h]
            o = oi_scr[rs, vs] + _dot_nt(qe_scr[rs, ks], s_old.astype(jnp.bfloat16))
            decay = jnp.exp(b_tail[SUBLANES - 1:SUBLANES, ks])
            state_t[h] = decay * (s_old + kvt_scr[c, h])
            o = _rms(o, ng_ref[...])
            gate = head_cols((g0_ref, g1_ref), rs, h).astype(jnp.float32)
            o_ref[rs, vs] = (o * (gate * jax.nn.sigmoid(gate))).astype(o_ref.dtype)
        return carry

    lax.fori_loop(0, nchunk, recur, 0, unroll=GLA_UNROLL)


def _gla(proj, w2, gate_b, norm_g, seq):
    nt = proj.shape[0]
    tm = min(GLA_TOKENS, seq)
    steps_per_seq = seq // tm
    return pl.pallas_call(
        functools.partial(_gla_kernel, steps_per_seq=steps_per_seq),
        grid=(nt // tm,),
        in_specs=[
            pl.BlockSpec((tm, 512), lambda i: (i, COL_GLA_Q // 512)),
            pl.BlockSpec((tm, 512), lambda i: (i, COL_GLA_K // 512)),
            pl.BlockSpec((tm, HALF), lambda i: (i, COL_GLA_V // HALF)),
            pl.BlockSpec((tm, HALF), lambda i: (i, COL_GLA_V // HALF + 1)),
            pl.BlockSpec((tm, HALF), lambda i: (i, COL_GLA_G // HALF)),
            pl.BlockSpec((tm, HALF), lambda i: (i, COL_GLA_G // HALF + 1)),
            pl.BlockSpec((tm, LANES), lambda i: (i, COL_LR // LANES)),
            pl.BlockSpec((LANES, 512), lambda i: (0, 0)),
            pl.BlockSpec((1, 512), lambda i: (0, 0)),
            pl.BlockSpec((1, GLA_DV), lambda i: (0, 0)),
        ],
        out_specs=pl.BlockSpec((tm, 1024), lambda i: (i, 0)),
        out_shape=jax.ShapeDtypeStruct((nt, 1024), jnp.bfloat16),
        scratch_shapes=[
            pltpu.VMEM((GLA_HEADS, GLA_DV, GLA_DK), jnp.float32),
            pltpu.VMEM((tm, tm), jnp.bfloat16),
            pltpu.VMEM((tm, GLA_HEADS * GLA_DK), jnp.bfloat16),
            pltpu.VMEM((tm, GLA_HEADS * GLA_DK), jnp.bfloat16),
            pltpu.VMEM((tm, GLA_HEADS * GLA_DK), jnp.float32),
            pltpu.VMEM((tm, GLA_HEADS * GLA_DV), jnp.float32),
            pltpu.VMEM((tm // GLA_CHUNK, GLA_HEADS, GLA_DV, GLA_DK), jnp.float32),
        ],
        compiler_params=_params("arbitrary"),
        name="gla",
    )(*([proj] * 7), w2, gate_b, norm_g)


def _merge_kernel(x_ref, ya_ref, yb_ref, yc_ref, ga_ref, gb_ref, gc_ref, wa_ref, wb_ref, wc_ref, wo_ref, g_ref, o_ref,
                  m_scr, *, nb):
    t = pl.program_id(0)

    @pl.when(t == 0)
    def _():
        m_scr[1] = jnp.zeros(m_scr.shape[1:], m_scr.dtype)

    def branch(gate_ref, y_ref, w_ref):
        return jax.nn.sigmoid(gate_ref[...].astype(jnp.float32)) * _dot(y_ref[...], w_ref[...])

    def matmuls(slot):
        merged = branch(ga_ref, ya_ref, wa_ref) + branch(gb_ref, yb_ref, wb_ref) + branch(gc_ref, yc_ref, wc_ref)
        m_scr[slot] = _dot(merged.astype(jnp.bfloat16), wo_ref[...])

    def finish(slot):
        o_ref[...] = x_ref[...] + _rms(m_scr[slot], g_ref[...])

    for parity in (0, 1):
        @pl.when((t < nb) & (t % 2 == parity))
        def _():
            finish(1 - parity)
            matmuls(parity)

    @pl.when(t == nb)
    def _():
        finish((nb - 1) % 2)


def _merge(x, ya, yb, yc, proj, wa, wb, wc, wo, g, l):
    nt, d = x.shape
    bm = min(MERGE_BLOCK_M, nt)
    nb = nt // bm
    c0 = COL_GATES // d

    def resident(shape):
        return pl.BlockSpec((None,) + shape, lambda t: (l, 0, 0), pipeline_mode=pl.Buffered(1))

    def cur(t):
        return jnp.minimum(t, nb - 1)

    def before(t):
        return jnp.maximum(t - 1, 0)

    def branch_in():
        return pl.BlockSpec((bm, 1024), lambda t: (cur(t), 0))

    return pl.pallas_call(
        functools.partial(_merge_kernel, nb=nb),
        grid=(nb + 1,),
        in_specs=[
            pl.BlockSpec((bm, d), lambda t: (before(t), 0)),
            branch_in(), branch_in(), branch_in(),
            pl.BlockSpec((bm, d), lambda t: (cur(t), c0)),
            pl.BlockSpec((bm, d), lambda t: (cur(t), c0 + 1)),
            pl.BlockSpec((bm, d), lambda t: (cur(t), c0 + 2)),
            resident((1024, d)), resident((1024, d)), resident((1024, d)), resident((d, d)),
            pl.BlockSpec((1, d), lambda t: (0, 0)),
        ],
        out_specs=pl.BlockSpec((bm, d), lambda t: (before(t), 0)),
        out_shape=jax.ShapeDtypeStruct((nt, d), jnp.float32),
        scratch_shapes=[pltpu.VMEM((2, bm, d), jnp.float32)],
        compiler_params=_params("arbitrary"),
        name="merge",
    )(x, ya, yb, yc, proj, proj, proj, wa, wb, wc, wo, g)


def _split_w_in(w_in):
    gates = w_in[..., MAIN_COLS:].astype(jnp.bfloat16)
    main = jnp.pad(w_in[..., :MAIN_COLS].astype(jnp.bfloat16), ((0, 0), (0, 0), (0, MAIN_WIDTH - MAIN_COLS)))
    return gates, main


def kernel(x, rel_bias, ffn1_pre_g, ffn1_post_g, ffn1_w_gate, ffn1_w_up, ffn1_w_down, mix_pre_g, mix_post_g, w_in, attn_sink, conv_w, conv_b, conv_ln_g, conv_ln_b, gla_gate_w2, gla_gate_b, gla_norm_g, w_a_up, w_b_up, w_c_up, w_out, ffn2_pre_g, ffn2_post_g, ffn2_w_gate, ffn2_w_up, ffn2_w_down):
    batch, seq, d = x.shape
    depth = w_in.shape[0]
    bf = lambda w: w.astype(jnp.bfloat16)
    row = lambda v: v.reshape(1, -1)
    bias = _band_bias(rel_bias)
    f1g, f1u, f1d = bf(ffn1_w_gate), bf(ffn1_w_up), bf(ffn1_w_down)
    f2g, f2u, f2d = bf(ffn2_w_gate), bf(ffn2_w_up), bf(ffn2_w_down)
    w_gates, w_main = _split_w_in(w_in)
    wa, wb, wc, wo = bf(w_a_up), bf(w_b_up), bf(w_c_up), bf(w_out)
    w2 = jnp.pad(bf(gla_gate_w2), ((0, 0), (0, LANES - GLA_RANK), (0, 0)))
    xt = x.reshape(batch * seq, d)
    for l in range(depth):
        xt, h_mix = _ffn(xt, row(ffn1_pre_g[l]), row(ffn1_post_g[l]), f1g, f1u, f1d, l, gnext=row(mix_pre_g[l]))
        proj = _proj(h_mix, w_gates, w_main, l)
        ya = _attention(proj, attn_sink[l], bias, seq)
        yb = _conv(proj, conv_w[l], row(conv_b[l]), row(conv_ln_g[l]), row(conv_ln_b[l]), seq)
        yc = _gla(proj, w2[l], row(gla_gate_b[l]), row(gla_norm_g[l]), seq)
        xt = _merge(xt, ya, yb, yc, proj, wa, wb, wc, wo, row(mix_post_g[l]), l)
        xt = _ffn(xt, row(ffn2_pre_g[l]), row(ffn2_post_g[l]), f2g, f2u, f2d, l)
    return xt.reshape(batch, seq, d)
```
